```python
import jax, jax.numpy as jnp
from jax import lax
import numpy as np

D_MODEL = 1024
BATCH = 4
SEQ = 4096
DEPTH = 4
DEC_BATCH = 128
DEC_SEQ = 8
PAST_LEN = 8192
PAGE_SIZE = 128

ROPE_THETA = 500000.0
NORM_EPS = 1e-6
NEG_INF = -1e30
SEL_FORCE = 1e30
Q_BLOCK = 128
MOBA_Q_BLOCK = 64

MOBA_HEADS = 8
MOBA_KV_HEADS = 2
MOBA_DH = 64
MOBA_HPG = MOBA_HEADS // MOBA_KV_HEADS
MOBA_ROT = MOBA_DH // 4
MOBA_BLOCK = 256
MOBA_TOPK = 3
MOBA_SCALE = MOBA_DH ** -0.5

MLA_HEADS = 8
MLA_NOPE = 64
MLA_ROPE = 32
MLA_V = 64
MLA_Q_LORA = 256
MLA_KV_LORA = 128
MLA_ROW = MLA_KV_LORA + MLA_ROPE
MLA_SCALE = (MLA_NOPE + MLA_ROPE) ** -0.5

NSA_HEADS = 8
NSA_DH = 64
NSA_ROT = NSA_DH // 4
NSA_CMP_LEN = 32
NSA_CMP_STRIDE = 16
NSA_SEL_BLOCK = 64
NSA_SEL_TOPN = 16
NSA_WINDOW = 512
NSA_PHI_HIDDEN = 128
NSA_SCALE = NSA_DH ** -0.5

D_FF = 4 * D_MODEL

IN_WIDTHS = (MOBA_HEADS * MOBA_DH, MOBA_KV_HEADS * MOBA_DH, MOBA_KV_HEADS * MOBA_DH,
             MLA_Q_LORA, MLA_KV_LORA, MLA_ROPE,
             NSA_HEADS * NSA_DH, NSA_DH, NSA_DH, NSA_DH, NSA_DH, NSA_DH, NSA_DH, 3 * NSA_HEADS)
D_IN = sum(IN_WIDTHS)
IN_OFFSETS = tuple(int(v) for v in np.cumsum(IN_WIDTHS)[:-1])

kernel_name = "hybrid_moba_mla_nsa_step"


def rmsnorm(x, g):
    xf = x.astype(jnp.float32)
    y = xf * lax.rsqrt(jnp.mean(xf * xf, axis=-1, keepdims=True) + NORM_EPS)
    return (y * g.astype(jnp.float32)).astype(x.dtype)


def masked_softmax(s, mask):
    s = jnp.where(mask, s.astype(jnp.float32), NEG_INF)
    e = jnp.where(mask, jnp.exp(s - jnp.max(s, axis=-1, keepdims=True)), 0.0)
    return e / jnp.maximum(jnp.sum(e, axis=-1, keepdims=True), 1e-30)


def rope(x, pos, rot_dim):
    half = rot_dim // 2
    inv_freq = ROPE_THETA ** (-2.0 * jnp.arange(half, dtype=jnp.float32) / rot_dim)
    ang = pos.astype(jnp.float32)[:, None] * inv_freq
    cos = jnp.cos(ang)[:, None, :]
    sin = jnp.sin(ang)[:, None, :]
    xf = x.astype(jnp.float32)
    x1, x2 = xf[..., :half], xf[..., half:rot_dim]
    out = jnp.concatenate([x1 * cos - x2 * sin, x2 * cos + x1 * sin, xf[..., rot_dim:]], axis=-1)
    return out.astype(x.dtype)


def block_ids(n_seq, n_blocks):
    i = jnp.arange(n_seq * n_blocks)
    return i // n_blocks, i % n_blocks


def token_projections(h, pos, P):
    n_seq, t_len, _ = h.shape
    (mq, mk, mv, cq, ckv, kr, nq, kc, vc, ks, vs, kw, vw, ng) = jnp.split(h @ P["w_in"], IN_OFFSETS, axis=-1)
    moba_q = rope(mq.reshape(n_seq, t_len, MOBA_HEADS, MOBA_DH), pos, MOBA_ROT)
    moba_k = rope(mk.reshape(n_seq, t_len, MOBA_KV_HEADS, MOBA_DH), pos, MOBA_ROT)
    moba_kv = jnp.stack([moba_k, mv.reshape(n_seq, t_len, MOBA_KV_HEADS, MOBA_DH)], axis=2)
    q = (rmsnorm(cq, P["mla_q_norm"]) @ P["mla_w_uq"]).reshape(n_seq, t_len, MLA_HEADS, MLA_NOPE + MLA_ROPE)
    w_uk = P["mla_w_uk"].reshape(MLA_KV_LORA, MLA_HEADS, MLA_NOPE)
    mla_q_lat = jnp.einsum("nthd,chd->nthc", q[..., :MLA_NOPE], w_uk)
    mla_q_rope = rope(q[..., MLA_NOPE:], pos, MLA_ROPE)
    k_rope = rope(kr[:, :, None, :], pos, MLA_ROPE)[:, :, 0]
    mla_row = jnp.concatenate([rmsnorm(ckv, P["mla_kv_norm"]), k_rope], axis=-1)
    nsa_q_raw = nq.reshape(n_seq, t_len, NSA_HEADS, NSA_DH)
    nsa_q = rope(nsa_q_raw, pos, NSA_ROT)
    ks = rope(ks[:, :, None, :], pos, NSA_ROT)[:, :, 0]
    kw = rope(kw[:, :, None, :], pos, NSA_ROT)[:, :, 0]
    nsa_row = jnp.stack([kc, vc, ks, vs], axis=2)
    win_row = jnp.stack([kw, vw], axis=2)
    nsa_gate = jax.nn.sigmoid(ng.astype(jnp.float32)).reshape(n_seq, t_len, 3, NSA_HEADS)
    return {"moba_q": moba_q, "moba_kv": moba_kv, "mla_q_lat": mla_q_lat, "mla_q_rope": mla_q_rope,
            "mla_row": mla_row, "nsa_q_raw": nsa_q_raw, "nsa_q": nsa_q, "nsa_row": nsa_row,
            "win_row": win_row, "nsa_gate": nsa_gate}


def moba_core(q, q_pos, past_kv, past_valid, own_kv, own_pos):
    n_q = q.shape[0]
    n_r = past_kv.shape[2]
    n_o = own_kv.shape[0]
    s_p = jnp.einsum("qhd,qhrkd->qhrk", q, past_kv[..., 0, :]).reshape(n_q, MOBA_HEADS, n_r * MOBA_BLOCK)
    m_p = jnp.broadcast_to(past_valid[..., None], (n_q, MOBA_HEADS, n_r, MOBA_BLOCK)).reshape(n_q, MOBA_HEADS, n_r * MOBA_BLOCK)
    qg = q.reshape(n_q, MOBA_KV_HEADS, MOBA_HPG, MOBA_DH)
    s_o = jnp.einsum("qgjd,kgd->qgjk", qg, own_kv[:, 0]).reshape(n_q, MOBA_HEADS, n_o)
    m_o = jnp.broadcast_to((own_pos[None, :] <= q_pos[:, None])[:, None, :], (n_q, MOBA_HEADS, n_o))
    p = masked_softmax(jnp.concatenate([s_p, s_o], axis=-1) * MOBA_SCALE,
                       jnp.concatenate([m_p, m_o], axis=-1)).astype(q.dtype)
    o_p = jnp.einsum("qhn,qhnd->qhd", p[..., :n_r * MOBA_BLOCK],
                     past_kv[..., 1, :].reshape(n_q, MOBA_HEADS, n_r * MOBA_BLOCK, MOBA_DH))
    o_o = jnp.einsum("qgjk,kgd->qgjd", p[..., n_r * MOBA_BLOCK:].reshape(n_q, MOBA_KV_HEADS, MOBA_HPG, n_o),
                     own_kv[:, 1]).reshape(n_q, MOBA_HEADS, MOBA_DH)
    return o_p + o_o


def moba_prompt(q, kv):
    n_seq, t_len = q.shape[:2]
    nb = -(-t_len // MOBA_BLOCK)
    kvb = jnp.pad(kv, ((0, 0), (0, nb * MOBA_BLOCK - t_len), (0, 0), (0, 0), (0, 0)))
    kvb = kvb.reshape(n_seq, nb, MOBA_BLOCK, 2, MOBA_KV_HEADS, MOBA_DH)
    kvg = kvb.transpose(0, 1, 4, 2, 3, 5)
    own_blk = jnp.arange(t_len) // MOBA_BLOCK
    k_sel = min(MOBA_TOPK, nb - 1)
    if k_sel > 0:
        k_mean = jnp.mean(kvb[:, :, :, 0].astype(jnp.float32), axis=2)
        s_gate = jnp.einsum("ntgjd,nbgd->ntgjb",
                            q.reshape(n_seq, t_len, MOBA_KV_HEADS, MOBA_HPG, MOBA_DH).astype(jnp.float32),
                            k_mean).reshape(n_seq, t_len, MOBA_HEADS, nb)
        fully_past = (jnp.arange(nb)[None, :] < own_blk[:, None])[None, :, None, :]
        _, sel = lax.top_k(jnp.where(fully_past, s_gate, NEG_INF), k_sel)
        valid = jnp.broadcast_to((jnp.arange(k_sel)[None, :] < own_blk[:, None])[None, :, None, :], sel.shape)
    else:
        sel = jnp.zeros((n_seq, t_len, MOBA_HEADS, 0), jnp.int32)
        valid = jnp.zeros(sel.shape, bool)
    nqb = t_len // MOBA_Q_BLOCK
    ns, cs = block_ids(n_seq, nqb)
    grp = (jnp.arange(MOBA_HEADS) // MOBA_HPG)[None, :, None]

    def item(args):
        n, c, q_b, sel_b, val_b = args
        kv_n = kvg[n]
        past_kv = kv_n[sel_b, grp]
        ob = (c * MOBA_Q_BLOCK) // MOBA_BLOCK
        own_kv = kv_n[ob].transpose(1, 2, 0, 3)
        q_pos = c * MOBA_Q_BLOCK + jnp.arange(MOBA_Q_BLOCK)
        own_pos = ob * MOBA_BLOCK + jnp.arange(MOBA_BLOCK)
        return moba_core(q_b, q_pos, past_kv, val_b, own_kv, own_pos)

    o = lax.map(item, (ns, cs,
                       q.reshape(n_seq * nqb, MOBA_Q_BLOCK, MOBA_HEADS, MOBA_DH),
                       sel.reshape(n_seq * nqb, MOBA_Q_BLOCK, MOBA_HEADS, k_sel),
                       valid.reshape(n_seq * nqb, MOBA_Q_BLOCK, MOBA_HEADS, k_sel)))
    return o.reshape(n_seq, t_len, MOBA_HEADS * MOBA_DH)


def moba_sample(q, kv_new, cache_moba, l, page_table):
    n_seq, s_len = q.shape[:2]
    m = PAST_LEN // MOBA_BLOCK
    k_sel = min(MOBA_TOPK, m)
    q_pos = PAST_LEN + jnp.arange(s_len)
    own_pos = m * MOBA_BLOCK + jnp.arange(PAST_LEN - m * MOBA_BLOCK + s_len)
    grp = (jnp.arange(MOBA_HEADS) // MOBA_HPG)[None, :, None]

    def item(args):
        q_b, kv_b, pt = args
        past = cache_moba[l, pt].reshape(PAST_LEN, 2, MOBA_KV_HEADS, MOBA_DH)
        own_kv = jnp.concatenate([past[m * MOBA_BLOCK:], kv_b], axis=0)
        if k_sel > 0:
            full = past[:m * MOBA_BLOCK].reshape(m, MOBA_BLOCK, 2, MOBA_KV_HEADS, MOBA_DH)
            k_mean = jnp.mean(full[:, :, 0].astype(jnp.float32), axis=1)
            s_gate = jnp.einsum("sgjd,bgd->sgjb",
                                q_b.reshape(s_len, MOBA_KV_HEADS, MOBA_HPG, MOBA_DH).astype(jnp.float32),
                                k_mean).reshape(s_len, MOBA_HEADS, m)
            _, sel = lax.top_k(s_gate, k_sel)
            past_kv = full.transpose(0, 3, 1, 2, 4)[sel, grp]
            valid = jnp.ones(sel.shape, bool)
        else:
            past_kv = jnp.zeros((s_len, MOBA_HEADS, 0, MOBA_BLOCK, 2, MOBA_DH), q_b.dtype)
            valid = jnp.zeros((s_len, MOBA_HEADS, 0), bool)
        return moba_core(q_b, q_pos, past_kv, valid, own_kv, own_pos)

    o = lax.map(item, (q, kv_new, page_table))
    return o.reshape(n_seq, s_len, MOBA_HEADS * MOBA_DH)


def mla_core(q_lat, q_rope, q_pos, rows, k_pos):
    ckv, kr = rows[:, :MLA_KV_LORA], rows[:, MLA_KV_LORA:]
    s = (jnp.einsum("qhc,kc->hqk", q_lat, ckv) + jnp.einsum("qhr,kr->hqk", q_rope, kr)) * MLA_SCALE
    p = masked_softmax(s, (k_pos[None, :] <= q_pos[:, None])[None])
    return jnp.einsum("hqk,kc->qhc", p.astype(rows.dtype), ckv)


def mla_up(o_lat, w_uv):
    n_seq, t_len = o_lat.shape[:2]
    o = jnp.einsum("nthc,chv->nthv", o_lat, w_uv.reshape(MLA_KV_LORA, MLA_HEADS, MLA_V))
    return o.reshape(n_seq, t_len, MLA_HEADS * MLA_V)


def mla_prompt(q_lat, q_rope, rows, w_uv):
    n_seq, t_len = q_lat.shape[:2]
    nqb = t_len // Q_BLOCK
    ns, cs = block_ids(n_seq, nqb)
    k_pos = jnp.arange(t_len)

    def item(args):
        n, c, ql, qr = args
        return mla_core(ql, qr, c * Q_BLOCK + jnp.arange(Q_BLOCK), rows[n], k_pos)

    o = lax.map(item, (ns, cs, q_lat.reshape(n_seq * nqb, Q_BLOCK, MLA_HEADS, MLA_KV_LORA),
                       q_rope.reshape(n_seq * nqb, Q_BLOCK, MLA_HEADS, MLA_ROPE)))
    return mla_up(o.reshape(n_seq, t_len, MLA_HEADS, MLA_KV_LORA), w_uv)


def mla_sample(q_lat, q_rope, rows_new, cache_mla, l, page_table, w_uv):
    s_len = q_lat.shape[1]
    q_pos = PAST_LEN + jnp.arange(s_len)
    k_pos = jnp.arange(PAST_LEN + s_len)

    def item(args):
        ql, qr, rn, pt = args
        rows = jnp.concatenate([cache_mla[l, pt].reshape(PAST_LEN, MLA_ROW), rn], axis=0)
        return mla_core(ql, qr, q_pos, rows, k_pos)

    return mla_up(lax.map(item, (q_lat, q_rope, rows_new, page_table)), w_uv)


def nsa_compress(rows, pe, w1, w2):
    t_len = rows.shape[-2]
    nc = (t_len - NSA_CMP_LEN) // NSA_CMP_STRIDE + 1
    idx = np.arange(nc)[:, None] * NSA_CMP_STRIDE + np.arange(NSA_CMP_LEN)[None, :]
    blocks = rows[..., idx, :] + pe
    z = blocks.reshape(blocks.shape[:-2] + (NSA_CMP_LEN * NSA_DH,))
    return jax.nn.silu(z @ w1) @ w2


def cmp_sel_overlap(nc, nsb):
    start = np.arange(nc)[:, None] * NSA_CMP_STRIDE
    j0 = np.arange(nsb)[None, :] * NSA_SEL_BLOCK
    return ((start < j0 + NSA_SEL_BLOCK) & (start + NSA_CMP_LEN > j0)).astype(np.float32)


def nsa_core(q_raw, q_rot, q_pos, gate, ck, cv, sel_kv, wk, wv, w_pos):
    n_q = q_raw.shape[0]
    nc = ck.shape[0]
    nsb = sel_kv.shape[0]
    c_end = jnp.arange(nc) * NSA_CMP_STRIDE + (NSA_CMP_LEN - 1)
    s_c = jnp.einsum("qhd,cd->qhc", q_raw, ck) * NSA_SCALE
    p_c = masked_softmax(s_c, (c_end[None, :] <= q_pos[:, None])[:, None, :])
    o_c = jnp.einsum("qhc,cd->qhd", p_c.astype(cv.dtype), cv)
    imp = jnp.sum(p_c, axis=1) @ cmp_sel_overlap(nc, nsb)
    blk = jnp.arange(nsb)[None, :]
    cur = (q_pos // NSA_SEL_BLOCK)[:, None]
    forced = (blk == 0) | (blk == cur) | (blk == cur - 1)
    avail = blk * NSA_SEL_BLOCK <= q_pos[:, None]
    score = jnp.where(avail, jnp.where(forced, SEL_FORCE, imp), NEG_INF)
    top_vals, top_idx = lax.top_k(score, min(NSA_SEL_TOPN, nsb))
    n_k = top_idx.shape[1]
    g = sel_kv[top_idx]
    k_pos_s = top_idx[..., None] * NSA_SEL_BLOCK + jnp.arange(NSA_SEL_BLOCK)
    m_s = (top_vals > 0.5 * NEG_INF)[..., None] & (k_pos_s <= q_pos[:, None, None])
    s_s = jnp.einsum("qhd,qkld->qhkl", q_rot, g[..., 0, :]).reshape(n_q, NSA_HEADS, n_k * NSA_SEL_BLOCK) * NSA_SCALE
    p_s = masked_softmax(s_s, m_s.reshape(n_q, 1, n_k * NSA_SEL_BLOCK))
    o_s = jnp.einsum("qhn,qnd->qhd", p_s.astype(g.dtype), g[..., 1, :].reshape(n_q, n_k * NSA_SEL_BLOCK, NSA_DH))
    dist = q_pos[:, None] - w_pos[None, :]
    m_w = (dist >= 0) & (dist <= NSA_WINDOW) & (w_pos[None, :] >= 0)
    s_w = jnp.einsum("qhd,ld->qhl", q_rot, wk) * NSA_SCALE
    p_w = masked_softmax(s_w, m_w[:, None, :])
    o_w = jnp.einsum("qhl,ld->qhd", p_w.astype(wv.dtype), wv)
    o = gate[:, 0, :, None] * o_c + gate[:, 1, :, None] * o_s + gate[:, 2, :, None] * o_w
    return o.astype(q_rot.dtype)


def nsa_prompt(tp, P):
    rows, win = tp["nsa_row"], tp["win_row"]
    n_seq, t_len = rows.shape[:2]
    ck = nsa_compress(rows[:, :, 0], P["nsa_pe_k"], P["nsa_w1_k"], P["nsa_w2_k"])
    cv = nsa_compress(rows[:, :, 1], P["nsa_pe_v"], P["nsa_w1_v"], P["nsa_w2_v"])
    nsb = -(-t_len // NSA_SEL_BLOCK)
    sel = jnp.pad(rows[:, :, 2:], ((0, 0), (0, nsb * NSA_SEL_BLOCK - t_len), (0, 0), (0, 0)))
    sel = sel.reshape(n_seq, nsb, NSA_SEL_BLOCK, 2, NSA_DH)
    wpad = jnp.pad(win, ((0, 0), (NSA_WINDOW, 0), (0, 0), (0, 0)))
    nqb = t_len // Q_BLOCK
    ns, cs = block_ids(n_seq, nqb)

    def item(args):
        n, c, qr, qo, g = args
        start = c * Q_BLOCK
        q_pos = start + jnp.arange(Q_BLOCK)
        w = lax.dynamic_slice_in_dim(wpad[n], start, Q_BLOCK + NSA_WINDOW, axis=0)
        w_pos = start - NSA_WINDOW + jnp.arange(Q_BLOCK + NSA_WINDOW)
        return nsa_core(qr, qo, q_pos, g, ck[n], cv[n], sel[n], w[:, 0], w[:, 1], w_pos)

    o = lax.map(item, (ns, cs,
                       tp["nsa_q_raw"].reshape(n_seq * nqb, Q_BLOCK, NSA_HEADS, NSA_DH),
                       tp["nsa_q"].reshape(n_seq * nqb, Q_BLOCK, NSA_HEADS, NSA_DH),
                       tp["nsa_gate"].reshape(n_seq * nqb, Q_BLOCK, 3, NSA_HEADS)))
    return o.reshape(n_seq, t_len, NSA_HEADS * NSA_DH)


def nsa_sample(tp, P, cache_nsa, win_buf, l, page_table):
    n_seq, s_len = tp["nsa_row"].shape[:2]
    wb = win_buf.shape[1]
    t_len = PAST_LEN + s_len
    nsb = -(-t_len // NSA_SEL_BLOCK)
    q_pos = PAST_LEN + jnp.arange(s_len)
    w_pos = PAST_LEN - wb + jnp.arange(wb + s_len)

    def item(args):
        qr, qo, g, rn, wn, buf, pt = args
        rows = jnp.concatenate([cache_nsa[l, pt].reshape(PAST_LEN, 4, NSA_DH), rn], axis=0)
        ck = nsa_compress(rows[:, 0], P["nsa_pe_k"], P["nsa_w1_k"], P["nsa_w2_k"])
        cv = nsa_compress(rows[:, 1], P["nsa_pe_v"], P["nsa_w1_v"], P["nsa_w2_v"])
        sel = jnp.pad(rows[:, 2:], ((0, nsb * NSA_SEL_BLOCK - t_len), (0, 0), (0, 0)))
        sel = sel.reshape(nsb, NSA_SEL_BLOCK, 2, NSA_DH)
        w = jnp.concatenate([buf, wn], axis=0)
        return nsa_core(qr, qo, q_pos, g, ck, cv, sel, w[:, 0], w[:, 1], w_pos)

    o = lax.map(item, (tp["nsa_q_raw"], tp["nsa_q"], tp["nsa_gate"], tp["nsa_row"], tp["win_row"],
                       win_buf, page_table))
    return o.reshape(n_seq, s_len, NSA_HEADS * NSA_DH)


def merge_branches(h, o_a, o_b, o_c, P):
    g = jax.nn.sigmoid((h @ P["w_gate"]).astype(jnp.float32))
    g_a, g_b, g_c = jnp.split(g, 3, axis=-1)
    merged = g_a * (o_a @ P["w_oa"]) + g_b * (o_b @ P["w_ob"]) + g_c * (o_c @ P["w_oc"])
    return merged.astype(h.dtype) @ P["w_out"]


def sq_relu_mlp(h, P):
    u = jax.nn.relu(h @ P["w_ff1"])
    return (u * u) @ P["w_ff2"]


def residual_layer(x, h, o_a, o_b, o_c, P):
    x = x + merge_branches(h, o_a, o_b, o_c, P)
    return x + sq_relu_mlp(rmsnorm(x, P["norm_ffn"]), P)


def setup_inputs(seed: int = 0) -> dict:
    key = jax.random.key(seed)
    ks = jax.random.split(key, 32)
    f32 = jnp.float32
    n_pages = PAST_LEN // PAGE_SIZE
    n_used = DEC_BATCH * n_pages
    n_pool = n_used + max(1, n_used // 4)
    win_len = min(NSA_WINDOW, PAST_LEN)

    def nrm(k, shape, scale=1.0):
        return jax.random.normal(k, shape, f32) * scale

    def gain(k, shape):
        return 1.0 + 0.05 * jax.random.normal(k, shape, f32)

    page_table = jax.random.permutation(ks[0], n_pool)[:n_used].reshape(DEC_BATCH, n_pages).astype(jnp.int32)
    mix_w = MOBA_HEADS * MOBA_DH
    return {
        "x_prompt": nrm(ks[1], (BATCH, SEQ, D_MODEL)),
        "x_sample": nrm(ks[2], (DEC_BATCH, DEC_SEQ, D_MODEL)),
        "cache_moba": nrm(ks[3], (DEPTH, n_pool, PAGE_SIZE, 2, MOBA_KV_HEADS, MOBA_DH)),
        "cache_mla": nrm(ks[4], (DEPTH, n_pool, PAGE_SIZE, MLA_ROW)),
        "cache_nsa": nrm(ks[5], (DEPTH, n_pool, PAGE_SIZE, 4, NSA_DH)),
        "state_nsa_win": nrm(ks[6], (DEPTH, DEC_BATCH, win_len, 2, NSA_DH)),
        "page_table": page_table,
        "norm_mix": gain(ks[7], (DEPTH, D_MODEL)),
        "w_in": nrm(ks[8], (DEPTH, D_MODEL, D_IN), D_MODEL ** -0.5),
        "mla_q_norm": gain(ks[9], (DEPTH, MLA_Q_LORA)),
        "mla_w_uq": nrm(ks[10], (DEPTH, MLA_Q_LORA, MLA_HEADS * (MLA_NOPE + MLA_ROPE)), MLA_Q_LORA ** -0.5),
        "mla_kv_norm": gain(ks[11], (DEPTH, MLA_KV_LORA)),
        "mla_w_uk": nrm(ks[12], (DEPTH, MLA_KV_LORA, MLA_HEADS * MLA_NOPE), MLA_KV_LORA ** -0.5),
        "mla_w_uv": nrm(ks[13], (DEPTH, MLA_KV_LORA, MLA_HEADS * MLA_V), MLA_KV_LORA ** -0.5),
        "nsa_pe_k": nrm(ks[14], (DEPTH, NSA_CMP_LEN, NSA_DH), 0.1),
        "nsa_w1_k": nrm(ks[15], (DEPTH, NSA_CMP_LEN * NSA_DH, NSA_PHI_HIDDEN), (NSA_CMP_LEN * NSA_DH) ** -0.5),
        "nsa_w2_k": nrm(ks[16], (DEPTH, NSA_PHI_HIDDEN, NSA_DH), NSA_PHI_HIDDEN ** -0.5),
        "nsa_pe_v": nrm(ks[17], (DEPTH, NSA_CMP_LEN, NSA_DH), 0.1),
        "nsa_w1_v": nrm(ks[18], (DEPTH, NSA_CMP_LEN * NSA_DH, NSA_PHI_HIDDEN), (NSA_CMP_LEN * NSA_DH) ** -0.5),
        "nsa_w2_v": nrm(ks[19], (DEPTH, NSA_PHI_HIDDEN, NSA_DH), NSA_PHI_HIDDEN ** -0.5),
        "w_oa": nrm(ks[20], (DEPTH, mix_w, D_MODEL), mix_w ** -0.5),
        "w_ob": nrm(ks[21], (DEPTH, MLA_HEADS * MLA_V, D_MODEL), (MLA_HEADS * MLA_V) ** -0.5),
        "w_oc": nrm(ks[22], (DEPTH, NSA_HEADS * NSA_DH, D_MODEL), (NSA_HEADS * NSA_DH) ** -0.5),
        "w_gate": nrm(ks[23], (DEPTH, D_MODEL, 3 * D_MODEL), D_MODEL ** -0.5),
        "w_out": nrm(ks[24], (DEPTH, D_MODEL, D_MODEL), D_MODEL ** -0.5),
        "norm_ffn": gain(ks[25], (DEPTH, D_MODEL)),
        "w_ff1": nrm(ks[26], (DEPTH, D_MODEL, D_FF), D_MODEL ** -0.5),
        "w_ff2": nrm(ks[27], (DEPTH, D_FF, D_MODEL), D_FF ** -0.5),
        "norm_final": gain(ks[28], (D_MODEL,)),
    }


def reference(x_prompt, x_sample, cache_moba, cache_mla, cache_nsa, state_nsa_win, page_table,
              norm_mix, w_in, mla_q_norm, mla_w_uq, mla_kv_norm, mla_w_uk, mla_w_uv,
              nsa_pe_k, nsa_w1_k, nsa_w2_k, nsa_pe_v, nsa_w1_v, nsa_w2_v,
              w_oa, w_ob, w_oc, w_gate, w_out, norm_ffn, w_ff1, w_ff2, norm_final):
    t_len = x_prompt.shape[1]
    s_len = x_sample.shape[1]
    pos_p = jnp.arange(t_len)
    pos_s = PAST_LEN + jnp.arange(s_len)
    xp, xs = x_prompt, x_sample
    moba_p, moba_s, mla_p, mla_s, nsa_p, nsa_s, win_p, win_s = [], [], [], [], [], [], [], []
    for l in range(DEPTH):
        P = {"norm_mix": norm_mix[l], "w_in": w_in[l], "mla_q_norm": mla_q_norm[l], "mla_w_uq": mla_w_uq[l],
             "mla_kv_norm": mla_kv_norm[l], "mla_w_uk": mla_w_uk[l], "mla_w_uv": mla_w_uv[l],
             "nsa_pe_k": nsa_pe_k[l], "nsa_w1_k": nsa_w1_k[l], "nsa_w2_k": nsa_w2_k[l],
             "nsa_pe_v": nsa_pe_v[l], "nsa_w1_v": nsa_w1_v[l], "nsa_w2_v": nsa_w2_v[l],
             "w_oa": w_oa[l], "w_ob": w_ob[l], "w_oc": w_oc[l], "w_gate": w_gate[l], "w_out": w_out[l],
             "norm_ffn": norm_ffn[l], "w_ff1": w_ff1[l], "w_ff2": w_ff2[l]}
        h = rmsnorm(xp, P["norm_mix"])
        tp = token_projections(h, pos_p, P)
        o_a = moba_prompt(tp["moba_q"], tp["moba_kv"])
        o_b = mla_prompt(tp["mla_q_lat"], tp["mla_q_rope"], tp["mla_row"], P["mla_w_uv"])
        o_c = nsa_prompt(tp, P)
        xp = residual_layer(xp, h, o_a, o_b, o_c, P)
        moba_p.append(tp["moba_kv"])
        mla_p.append(tp["mla_row"])
        nsa_p.append(tp["nsa_row"])
        win_p.append(tp["win_row"][:, t_len - min(NSA_WINDOW, t_len):])
        h = rmsnorm(xs, P["norm_mix"])
        ts = token_projections(h, pos_s, P)
        buf = state_nsa_win[l]
        o_a = moba_sample(ts["moba_q"], ts["moba_kv"], cache_moba, l, page_table)
        o_b = mla_sample(ts["mla_q_lat"], ts["mla_q_rope"], ts["mla_row"], cache_mla, l, page_table, P["mla_w_uv"])
        o_c = nsa_sample(ts, P, cache_nsa, buf, l, page_table)
        xs = residual_layer(xs, h, o_a, o_b, o_c, P)
        moba_s.append(ts["moba_kv"])
        mla_s.append(ts["mla_row"])
        nsa_s.append(ts["nsa_row"])
        win_full = jnp.concatenate([buf, ts["win_row"]], axis=1)
        win_s.append(win_full[:, win_full.shape[1] - min(NSA_WINDOW, win_full.shape[1]):])
    y_prompt = rmsnorm(xp, norm_final)
    y_sample = rmsnorm(xs, norm_final)
    return (y_prompt, y_sample, jnp.stack(moba_p), jnp.stack(moba_s), jnp.stack(mla_p), jnp.stack(mla_s),
            jnp.stack(nsa_p), jnp.stack(nsa_s), jnp.stack(win_p), jnp.stack(win_s))
```

```python
import collections
import functools

import numpy as np
import jax
import jax.numpy as jnp
from jax import lax
from jax.experimental import pallas as pl
from jax.experimental.pallas import tpu as pltpu

F32 = jnp.float32
BF16 = jnp.bfloat16

D_MODEL = 1024
BATCH = 4
SEQ = 4096
DEPTH = 4
DEC_BATCH = 128
DEC_SEQ = 8
PAST_LEN = 8192
PAGE_SIZE = 128

ROPE_THETA = 500000.0
NORM_EPS = 1e-6
NEG_INF = -1e30
SEL_FORCE = 1e30

HEADS = 8
DH = 64
LANES = 128
MOBA_KV_HEADS = 2
MOBA_HPG = HEADS // MOBA_KV_HEADS
MOBA_BLOCK = 256
MOBA_TOPK = 3
MOBA_SCALE = DH ** -0.5
ROT16 = 16

MLA_NOPE = 64
MLA_ROPE = 32
MLA_Q_LORA = 256
MLA_KV_LORA = 128
MLA_ROW = MLA_KV_LORA + MLA_ROPE
MLA_QW = 256
MLA_SCALE = (MLA_NOPE + MLA_ROPE) ** -0.5

NSA_CMP_LEN = 32
NSA_CMP_STRIDE = 16
NSA_SEL_BLOCK = 64
SEL_SHIFT = 6
NSA_SEL_TOPN = 16
NSA_WINDOW = 512
NSA_PHI_HIDDEN = 128
NSA_SCALE = DH ** -0.5

D_FF = 4 * D_MODEL

IN_WIDTHS = (HEADS * DH, MOBA_KV_HEADS * DH, MOBA_KV_HEADS * DH, MLA_Q_LORA, MLA_KV_LORA, MLA_ROPE,
             HEADS * DH, DH, DH, DH, DH, DH, DH, 3 * HEADS)
(S_MQ, S_MK, S_MV, S_CQ, S_CKV, S_KR, S_NQ, S_KC, S_VC, S_KS, S_VS, S_KW, S_VW, S_NG) = (
    int(v) for v in np.concatenate([[0], np.cumsum(IN_WIDTHS)[:-1]]))

G_MQ, G_MQR = 0, 1024
G_MKV, G_MKVR = 2048, 2304
G_CQ, G_CKV = 2560, 2816
G_KR, G_KRR = 2944, 3072
G_NQ, G_NQR = 3200, 4224
G_NSA, G_NSAR = 5248, 5504
G_WIN, G_WINR = 5760, 5888
G_NG = 6016
W_ALL_COLS = 6144

VMEM_LIMIT = 56 * 1024 * 1024

Cfg = collections.namedtuple("Cfg", "n_seq t_len dec_batch dec_seq past_len page")
PROD = Cfg(BATCH, SEQ, DEC_BATCH, DEC_SEQ, PAST_LEN, PAGE_SIZE)


def _params(n_axes):
    return pltpu.CompilerParams(dimension_semantics=("arbitrary",) * n_axes, vmem_limit_bytes=VMEM_LIMIT)


def _rms(x, g):
    return x * lax.rsqrt(jnp.mean(x * x, axis=-1, keepdims=True) + NORM_EPS) * g


def _mm(a, b):
    return jnp.dot(a.astype(BF16), b.astype(BF16), preferred_element_type=F32)


def _mm_nt(a, b):
    return lax.dot_general(a.astype(BF16), b.astype(BF16), (((1,), (1,)), ((), ())),
                           preferred_element_type=F32)


def _sigmoid(x):
    return 1.0 / (1.0 + jnp.exp(-x))


def _tile_lanes(t, n):
    return t if n == 1 else jnp.concatenate([t] * n, axis=1)


def _tile_rows(t, n):
    return t if n == 1 else jnp.concatenate([t] * n, axis=0)


def _iota(shape, dim):
    return lax.broadcasted_iota(jnp.int32, shape, dim)


def _const_spec(shape):
    nd = len(shape)
    return pl.BlockSpec(shape, lambda *_: (0,) * nd)


def _in_proj_kernel(x_ref, g_ref, wall_ref, qn_ref, wuq_ref, wcat_ref, kvn_ref,
                    ca_ref, sa_ref, cb_ref, sb_ref,
                    mq_ref, mkv_ref, qcat_ref, kcat_ref, nqraw_ref, nqrot_ref, cmp_ref, slc_ref, win_ref, gate_ref,
                    *, kv_group_major):
    hb = _rms(x_ref[...], g_ref[...]).astype(BF16)

    def proj(lo, width):
        return jnp.dot(hb, wall_ref[:, lo:lo + width], preferred_element_type=F32)

    ca, sa, cb, sb = ca_ref[...], sa_ref[...], cb_ref[...], sb_ref[...]
    rows = hb.shape[0]

    def rope16(z, zr, n_blocks, key_lanes=None):
        out = z * _tile_lanes(ca, n_blocks) + zr * _tile_lanes(sa, n_blocks)
        if key_lanes is None:
            return out
        return jnp.where(key_lanes(_iota((rows, n_blocks * LANES), 1)), out, z)

    mq_ref[...] = rope16(proj(G_MQ, 1024), proj(G_MQR, 1024), 8)
    if kv_group_major:
        k_lanes = lambda lane: (lane & (LANES - 1)) < DH
    else:
        k_lanes = lambda lane: lane < LANES
    mkv_ref[...] = rope16(proj(G_MKV, 256), proj(G_MKVR, 256), 2, k_lanes)

    cqn = _rms(proj(G_CQ, MLA_Q_LORA), qn_ref[...]).astype(BF16)
    qall = jnp.dot(cqn, wuq_ref[...], preferred_element_type=F32)
    q_rope = qall[:, 512:768] * _tile_lanes(cb, 2) + qall[:, 768:1024] * _tile_lanes(sb, 2)
    qin = jnp.concatenate([qall[:, :512], q_rope], axis=1).astype(BF16)
    qcat_ref[...] = jnp.dot(qin, wcat_ref[...], preferred_element_type=F32)
    kcat_ref[:, 0:LANES] = _rms(proj(G_CKV, MLA_KV_LORA), kvn_ref[...])
    kcat_ref[:, LANES:2 * LANES] = proj(G_KR, LANES) * cb + proj(G_KRR, LANES) * sb

    nq = proj(G_NQ, 1024)
    nqraw_ref[...] = nq
    nqrot_ref[...] = rope16(nq, proj(G_NQR, 1024), 8)
    key_half = lambda lane: lane < DH
    cmp_ref[...] = proj(G_NSA, LANES)
    slc_ref[...] = rope16(proj(G_NSA + LANES, LANES), proj(G_NSAR, LANES), 1, key_half)
    win_ref[...] = rope16(proj(G_WIN, LANES), proj(G_WINR, LANES), 1, key_half)
    gate_ref[...] = _sigmoid(proj(G_NG, LANES))


def in_proj(x, g_mix, w_all, q_norm, w_uq_all, w_cat, kv_norm, tabs, *, tm, kv_group_major):
    m = x.shape[0]
    ca, sa, cb, sb = tabs
    n_per = ca.shape[0] // tm
    row = lambda w: pl.BlockSpec((tm, w), lambda i: (i, 0))
    tab = pl.BlockSpec((tm, LANES), lambda i: (i % n_per, 0))
    widths = (1024, 256, HEADS * MLA_QW, 256, 1024, 1024, LANES, LANES, LANES, LANES)
    return pl.pallas_call(
        functools.partial(_in_proj_kernel, kv_group_major=kv_group_major),
        grid=(m // tm,),
        in_specs=[row(D_MODEL), _const_spec((1, D_MODEL)), _const_spec((D_MODEL, W_ALL_COLS)),
                  _const_spec((1, MLA_Q_LORA)), _const_spec((MLA_Q_LORA, 1024)),
                  _const_spec((768, HEADS * MLA_QW)), _const_spec((1, MLA_KV_LORA)), tab, tab, tab, tab],
        out_specs=[row(w) for w in widths],
        out_shape=[jax.ShapeDtypeStruct((m, w), F32) for w in widths],
        compiler_params=_params(1),
        name="in_proj",
    )(x, g_mix, w_all, q_norm, w_uq_all, w_cat, kv_norm, ca, sa, cb, sb)


def _merge_kernel(x_ref, g_ref, wgate_ref, oa_ref, ob_ref, oc_ref, wuv_ref, woa_ref, wob_ref, woc_ref,
                  wout_ref, out_ref):
    x = x_ref[...]
    hb = _rms(x, g_ref[...]).astype(BF16)

    def gate(k):
        return _sigmoid(jnp.dot(hb, wgate_ref[:, k * D_MODEL:(k + 1) * D_MODEL], preferred_element_type=F32))

    o_b = _mm(ob_ref[...], wuv_ref[...])
    merged = gate(0) * _mm(oa_ref[...], woa_ref[...])
    merged = merged + gate(1) * _mm(o_b, wob_ref[...])
    merged = merged + gate(2) * _mm(oc_ref[...], woc_ref[...])
    out_ref[...] = x + _mm(merged, wout_ref[...])


def merge(x, g_mix, w_gate, o_a, o_b_lat, o_c, w_uv_bd, w_oa, w_ob, w_oc, w_out, *, tm):
    m = x.shape[0]
    row = lambda w: pl.BlockSpec((tm, w), lambda i: (i, 0))
    return pl.pallas_call(
        _merge_kernel,
        grid=(m // tm,),
        in_specs=[row(D_MODEL), _const_spec((1, D_MODEL)), _const_spec((D_MODEL, 3 * D_MODEL)),
                  row(1024), row(1024), row(1024), _const_spec((1024, 512)),
                  _const_spec((1024, D_MODEL)), _const_spec((512, D_MODEL)), _const_spec((1024, D_MODEL)),
                  _const_spec((D_MODEL, D_MODEL))],
        out_specs=row(D_MODEL),
        out_shape=jax.ShapeDtypeStruct((m, D_MODEL), F32),
        compiler_params=_params(1),
        name="merge",
    )(x, g_mix, w_gate, o_a, o_b_lat, o_c, w_uv_bd, w_oa, w_ob, w_oc, w_out)


def _ffn_kernel(x_ref, g_ref, w1_ref, w2_ref, gf_ref, out_ref, *, final_norm):
    x = x_ref[...]
    hb = _rms(x, g_ref[...]).astype(BF16)
    acc = x
    for c in range(D_FF // D_MODEL):
        u = jnp.maximum(jnp.dot(hb, w1_ref[:, c * D_MODEL:(c + 1) * D_MODEL], preferred_element_type=F32), 0.0)
        acc = acc + jnp.dot((u * u).astype(BF16), w2_ref[c * D_MODEL:(c + 1) * D_MODEL, :],
                            preferred_element_type=F32)
    out_ref[...] = _rms(acc, gf_ref[...]) if final_norm else acc


def ffn(x, g_ffn, w1, w2, g_final, *, tm, final_norm):
    m = x.shape[0]
    row = pl.BlockSpec((tm, D_MODEL), lambda i: (i, 0))
    return pl.pallas_call(
        functools.partial(_ffn_kernel, final_norm=final_norm),
        grid=(m // tm,),
        in_specs=[row, _const_spec((1, D_MODEL)), _const_spec((D_MODEL, D_FF)), _const_spec((D_FF, D_MODEL)),
                  _const_spec((1, D_MODEL))],
        out_specs=row,
        out_shape=jax.ShapeDtypeStruct((m, D_MODEL), F32),
        compiler_params=_params(1),
        name="ffn",
    )(x, g_ffn, w1, w2, g_final)


def _stack_heads(ref, width):
    return jnp.concatenate([ref[:, h * width:(h + 1) * width] for h in range(HEADS)], axis=0)


def _unstack_heads(out_ref, o, rows):
    for h in range(HEADS):
        out_ref[:, h * LANES:(h + 1) * LANES] = o[h * rows:(h + 1) * rows, :]


def _softmax_parts(parts):
    m = None
    for s, mask, _, _ in parts:
        mt = jnp.max(jnp.where(mask, s, NEG_INF), axis=1, keepdims=True)
        m = mt if m is None else jnp.maximum(m, mt)
    l, es = None, []
    for s, mask, _, _ in parts:
        e = jnp.where(mask, jnp.exp(s - m), 0.0)
        es.append(e)
        lt = jnp.sum(e, axis=1, keepdims=True)
        l = lt if l is None else l + lt
    inv = 1.0 / jnp.maximum(l, 1e-30)
    acc, ps = None, []
    for e, (_, _, v, v_t) in zip(es, parts):
        p = e * inv
        ps.append(p)
        o = _mm_nt(p, v) if v_t else _mm(p, v)
        acc = o if acc is None else acc + o
    return acc, ps


def _online_init(m_sc, l_sc, acc_sc):
    m_sc[...] = jnp.full(m_sc.shape, NEG_INF, F32)
    l_sc[...] = jnp.zeros(l_sc.shape, F32)
    acc_sc[...] = jnp.zeros(acc_sc.shape, F32)


def _online_step(m_sc, l_sc, acc_sc, s, mask, v):
    m_old = m_sc[...]
    m_new = jnp.maximum(m_old, jnp.max(jnp.where(mask, s, NEG_INF), axis=1, keepdims=True))
    e = jnp.where(mask, jnp.exp(s - m_new), 0.0)
    alpha = jnp.exp(m_old - m_new)
    l_sc[...] = alpha * l_sc[...] + jnp.sum(e, axis=1, keepdims=True)
    acc_sc[...] = alpha * acc_sc[...] + _mm(e, v)
    m_sc[...] = m_new


def _online_finish(l_sc, acc_sc):
    return acc_sc[...] / jnp.maximum(l_sc[...], 1e-30)


def _rank_before(score, n_cols):
    lane = _iota(score.shape, 1)
    cnt = jnp.zeros(score.shape, F32)
    for j in range(n_cols):
        col = score[:, j:j + 1]
        ahead = (col > score) | ((col == score) & (lane > j))
        cnt = cnt + jnp.where(ahead, 1.0, 0.0)
    return cnt


def _mla_prompt_kernel(q_ref, k_ref, o_ref, m_sc, l_sc, acc_sc, *, tq, tk):
    i = pl.program_id(1)
    q = _stack_heads(q_ref, MLA_QW).astype(BF16)
    rows = HEADS * tq
    _online_init(m_sc, l_sc, acc_sc)
    q_pos = i * tq + (_iota((rows, tk), 0) & (tq - 1))
    col = _iota((rows, tk), 1)

    def body(j, carry):
        kt = k_ref[pl.ds(pl.multiple_of(j * tk, tk), tk), :].astype(BF16)
        s = _mm_nt(q, kt) * MLA_SCALE
        _online_step(m_sc, l_sc, acc_sc, s, j * tk + col <= q_pos, kt[:, :MLA_KV_LORA])
        return carry

    lax.fori_loop(0, (i * tq + tq - 1) // tk + 1, body, 0)
    _unstack_heads(o_ref, _online_finish(l_sc, acc_sc), tq)


def mla_prompt(qcat, kcat, *, tq=128, tk=512):
    n, t, _ = qcat.shape
    tk = min(tk, t)
    rows = HEADS * tq
    return pl.pallas_call(
        functools.partial(_mla_prompt_kernel, tq=tq, tk=tk),
        grid=(n, t // tq),
        in_specs=[pl.BlockSpec((None, tq, HEADS * MLA_QW), lambda b, i: (b, i, 0)),
                  pl.BlockSpec((None, t, MLA_QW), lambda b, i: (b, 0, 0))],
        out_specs=pl.BlockSpec((None, tq, HEADS * LANES), lambda b, i: (b, i, 0)),
        out_shape=jax.ShapeDtypeStruct((n, t, HEADS * LANES), F32),
        scratch_shapes=[pltpu.VMEM((rows, 1), F32), pltpu.VMEM((rows, 1), F32),
                        pltpu.VMEM((rows, MLA_KV_LORA), F32)],
        compiler_params=_params(2),
        name="mla_prompt",
    )(qcat, kcat)


def _moba_prompt_kernel(q_ref, kv_ref, o_ref, m_sc, l_sc, acc_sc, *, n_blocks):
    i = pl.program_id(2)
    tq = MOBA_BLOCK
    rows = MOBA_HPG * tq
    q = jnp.concatenate([q_ref[:, h * LANES:(h + 1) * LANES] for h in range(MOBA_HPG)], axis=0)
    qb = q.astype(BF16)
    width = LANES
    k_mean = jnp.concatenate(
        [jnp.mean(kv_ref[j * tq:(j + 1) * tq, :], axis=0, keepdims=True) for j in range(n_blocks)]
        + [jnp.zeros((width - n_blocks, LANES), F32)], axis=0)
    s_gate = _mm_nt(q, k_mean)
    blk = _iota((rows, width), 1)
    past = blk < i
    cnt = _rank_before(jnp.where(past, s_gate, NEG_INF), n_blocks)
    sel = jnp.where(past & (cnt < MOBA_TOPK), 1.0, 0.0)

    _online_init(m_sc, l_sc, acc_sc)
    q_in = _iota((rows, tq), 0) & (tq - 1)
    col = _iota((rows, tq), 1)
    kv_own = kv_ref[pl.ds(pl.multiple_of(i * tq, tq), tq), :].astype(BF16)
    _online_step(m_sc, l_sc, acc_sc, _mm_nt(qb, kv_own) * MOBA_SCALE, col <= q_in, kv_own)
    for j in range(n_blocks - 1):
        @pl.when(j < i)
        def _():
            kv_j = kv_ref[j * tq:(j + 1) * tq, :].astype(BF16)
            vis = jnp.broadcast_to(sel[:, j:j + 1], (rows, tq)) > 0.5
            _online_step(m_sc, l_sc, acc_sc, _mm_nt(qb, kv_j) * MOBA_SCALE, vis, kv_j)
    o = _online_finish(l_sc, acc_sc)
    for h in range(MOBA_HPG):
        o_ref[:, h * LANES:(h + 1) * LANES] = o[h * tq:(h + 1) * tq, :]


def moba_prompt(mq, mkv):
    n, t, _ = mq.shape
    n_blocks = t // MOBA_BLOCK
    rows = MOBA_HPG * MOBA_BLOCK
    gw = MOBA_HPG * LANES
    return pl.pallas_call(
        functools.partial(_moba_prompt_kernel, n_blocks=n_blocks),
        grid=(n, MOBA_KV_HEADS, n_blocks),
        in_specs=[pl.BlockSpec((None, MOBA_BLOCK, gw), lambda b, g, i: (b, i, g)),
                  pl.BlockSpec((None, t, LANES), lambda b, g, i: (b, 0, g))],
        out_specs=pl.BlockSpec((None, MOBA_BLOCK, gw), lambda b, g, i: (b, i, g)),
        out_shape=jax.ShapeDtypeStruct((n, t, HEADS * LANES), F32),
        scratch_shapes=[pltpu.VMEM((rows, 1), F32), pltpu.VMEM((rows, 1), F32), pltpu.VMEM((rows, LANES), F32)],
        compiler_params=_params(3),
        name="moba_prompt",
    )(mq, mkv)


def _compress(load_rows, n_chunks, pe_ref, w1_ref, w2_ref):
    lo = jnp.zeros((n_chunks, 2 * NSA_PHI_HIDDEN), F32)
    hi = jnp.zeros((n_chunks, 2 * NSA_PHI_HIDDEN), F32)
    for r in range(NSA_CMP_STRIDE):
        x = load_rows(r)
        lo = lo + _mm(x + pe_ref[r:r + 1, :], w1_ref[r])
        hi = hi + _mm(x + pe_ref[NSA_CMP_STRIDE + r:NSA_CMP_STRIDE + r + 1, :], w1_ref[NSA_CMP_STRIDE + r])
    hidden = lo + pltpu.roll(hi, n_chunks - 1, 0)
    return _mm(hidden * _sigmoid(hidden), w2_ref[...])


def _nsa_compress_kernel(rows_ref, pe_ref, w1_ref, w2_ref, out_ref, *, n_chunks):
    out_ref[...] = _compress(lambda r: rows_ref[pl.ds(r, n_chunks, stride=NSA_CMP_STRIDE), :],
                             n_chunks, pe_ref, w1_ref, w2_ref)


def nsa_compress_prompt(cmp_rows, pe, w1, w2):
    n, t, _ = cmp_rows.shape
    n_chunks = t // NSA_CMP_STRIDE
    return pl.pallas_call(
        functools.partial(_nsa_compress_kernel, n_chunks=n_chunks),
        grid=(n,),
        in_specs=[pl.BlockSpec((None, t, LANES), lambda b: (b, 0, 0)), _const_spec(pe.shape),
                  _const_spec(w1.shape), _const_spec(w2.shape)],
        out_specs=pl.BlockSpec((None, n_chunks, LANES), lambda b: (b, 0, 0)),
        out_shape=jax.ShapeDtypeStruct((n, n_chunks, LANES), F32),
        compiler_params=_params(1),
        name="nsa_compress",
    )(cmp_rows, pe, w1, w2)


def _nsa_select(p_c, q_pos, ovl_ref, rows_q, width, n_sel_blocks):
    p_sum = p_c[0:rows_q, :]
    for h in range(1, HEADS):
        p_sum = p_sum + p_c[h * rows_q:(h + 1) * rows_q, :]
    imp = _mm(p_sum, ovl_ref[...])
    blk = _iota((rows_q, width), 1)
    cur = q_pos >> SEL_SHIFT
    forced = (blk == 0) | (blk == cur) | (blk == cur - 1)
    avail = (blk * NSA_SEL_BLOCK <= q_pos) & (blk < n_sel_blocks)
    score = jnp.where(avail, jnp.where(forced, SEL_FORCE, imp), NEG_INF)
    return avail & (_rank_before(score, n_sel_blocks) < NSA_SEL_TOPN)


def _nsa_prompt_kernel(qraw_ref, qrot_ref, gate_ref, ckcv_ref, rows_ref, win_ref, ovl_ref, exp_ref, o_ref,
                       m_sc, l_sc, acc_sc, *, tq, tk, n_cmp, n_sel_blocks):
    i = pl.program_id(1)
    rows = HEADS * tq
    q_raw = _stack_heads(qraw_ref, LANES)
    q_rot = _stack_heads(qrot_ref, LANES).astype(BF16)
    nc_pad = ckcv_ref.shape[0]

    ckcv = ckcv_ref[...].astype(BF16)
    q_pos_c = i * tq + (_iota((rows, nc_pad), 0) & (tq - 1))
    c_idx = _iota((rows, nc_pad), 1)
    vis_c = (c_idx * NSA_CMP_STRIDE + (NSA_CMP_LEN - 1) <= q_pos_c) & (c_idx < n_cmp)
    o_c, (p_c,) = _softmax_parts([(_mm_nt(q_raw, ckcv) * NSA_SCALE, vis_c, ckcv, False)])

    width = ovl_ref.shape[1]
    q_pos1 = i * tq + _iota((tq, width), 0)
    sel = _nsa_select(p_c, q_pos1, ovl_ref, tq, width, n_sel_blocks)
    sel_b = jnp.where(sel, 1.0, 0.0).astype(BF16)
    _online_init(m_sc, l_sc, acc_sc)
    q_pos_k = i * tq + (_iota((rows, tk), 0) & (tq - 1))
    col_k = _iota((rows, tk), 1)

    def body(t, carry):
        kv = rows_ref[pl.ds(pl.multiple_of(t * tk, tk), tk), :].astype(BF16)
        picked = jnp.dot(sel_b, exp_ref[t], preferred_element_type=F32)
        vis = (_tile_rows(picked, HEADS) > 0.5) & (t * tk + col_k <= q_pos_k)
        _online_step(m_sc, l_sc, acc_sc, _mm_nt(q_rot, kv) * NSA_SCALE, vis, kv)
        return carry

    lax.fori_loop(0, (i * tq + tq - 1) // tk + 1, body, 0)
    o_s = _online_finish(l_sc, acc_sc)

    q_pos_w = i * tq + (_iota((rows, tq), 0) & (tq - 1))
    col_w = _iota((rows, tq), 1)
    parts = []
    for w in range(NSA_WINDOW // tq + 1):
        start = (i - w) * tq
        kv = win_ref[pl.ds(pl.multiple_of(jnp.maximum(start, 0), tq), tq), :].astype(BF16)
        k_pos = start + col_w
        dist = q_pos_w - k_pos
        vis = (dist >= 0) & (dist <= NSA_WINDOW) & (k_pos >= 0)
        parts.append((_mm_nt(q_rot, kv) * NSA_SCALE, vis, kv, False))
    o_w, _ = _softmax_parts(parts)

    gate = gate_ref[...]
    for h in range(HEADS):
        sl = slice(h * tq, (h + 1) * tq)
        o_ref[:, h * LANES:(h + 1) * LANES] = (gate[:, h:h + 1] * o_c[sl]
                                               + gate[:, HEADS + h:HEADS + h + 1] * o_s[sl]
                                               + gate[:, 2 * HEADS + h:2 * HEADS + h + 1] * o_w[sl])


def _overlap_matrix(nc_pad, n_cmp, width):
    start = np.arange(nc_pad)[:, None] * NSA_CMP_STRIDE
    j0 = np.arange(width)[None, :] * NSA_SEL_BLOCK
    ovl = (start < j0 + NSA_SEL_BLOCK) & (start + NSA_CMP_LEN > j0) & (np.arange(nc_pad)[:, None] < n_cmp)
    return jnp.asarray(ovl, BF16)


def _expand_matrix(n_tiles, width, tk):
    key_blk = (np.arange(n_tiles)[:, None] * tk + np.arange(tk)[None, :]) // NSA_SEL_BLOCK
    return jnp.asarray(key_blk[:, None, :] == np.arange(width)[None, :, None], BF16)


def nsa_prompt(nq_raw, nq_rot, gate, ckcv, slc_rows, win_rows, *, tq=128, tk=256):
    n, t, _ = nq_raw.shape
    tk = min(tk, t)
    nc_pad = t // NSA_CMP_STRIDE
    n_cmp = (t - NSA_CMP_LEN) // NSA_CMP_STRIDE + 1
    n_sel_blocks = t // NSA_SEL_BLOCK
    width = max(LANES, n_sel_blocks)
    ovl = _overlap_matrix(nc_pad, n_cmp, width)
    expand = _expand_matrix(t // tk, width, tk)
    rows = HEADS * tq
    qspec = pl.BlockSpec((None, tq, HEADS * LANES), lambda b, i: (b, i, 0))
    return pl.pallas_call(
        functools.partial(_nsa_prompt_kernel, tq=tq, tk=tk, n_cmp=n_cmp, n_sel_blocks=n_sel_blocks),
        grid=(n, t // tq),
        in_specs=[qspec, qspec, pl.BlockSpec((None, tq, LANES), lambda b, i: (b, i, 0)),
                  pl.BlockSpec((None, nc_pad, LANES), lambda b, i: (b, 0, 0)),
                  pl.BlockSpec((None, t, LANES), lambda b, i: (b, 0, 0)),
                  pl.BlockSpec((None, t, LANES), lambda b, i: (b, 0, 0)),
                  _const_spec(ovl.shape), _const_spec(expand.shape)],
        out_specs=qspec,
        out_shape=jax.ShapeDtypeStruct((n, t, HEADS * LANES), F32),
        scratch_shapes=[pltpu.VMEM((rows, 1), F32), pltpu.VMEM((rows, 1), F32), pltpu.VMEM((rows, LANES), F32)],
        compiler_params=_params(2),
        name="nsa_prompt",
    )(nq_raw, nq_rot, gate, ckcv, slc_rows, win_rows, ovl, expand)


def _pad_rows(x, n):
    return jnp.concatenate([x, jnp.zeros((n - x.shape[0], x.shape[1]), x.dtype)], axis=0)


def _paged_attend(n_pages, page, s_sc, score_fn, vis_fn, value_t_fn, s_new, vis_new, v_new):
    for p in range(n_pages):
        s, vis = score_fn(p), vis_fn(p)
        s_sc[:, p * page:(p + 1) * page] = s if vis is None else jnp.where(vis, s, NEG_INF)
    m = jnp.maximum(jnp.max(s_sc[...], axis=1, keepdims=True),
                    jnp.max(jnp.where(vis_new, s_new, NEG_INF), axis=1, keepdims=True))
    e_new = jnp.where(vis_new, jnp.exp(s_new - m), 0.0)
    l = jnp.sum(e_new, axis=1, keepdims=True)
    acc = _mm(e_new, v_new)
    for p in range(n_pages):
        e, vis = jnp.exp(s_sc[:, p * page:(p + 1) * page] - m), vis_fn(p)
        if vis is not None:
            e = jnp.where(vis, e, 0.0)
        l = l + jnp.sum(e, axis=1, keepdims=True)
        acc = acc + _mm_nt(e, value_t_fn(p))
    return acc / jnp.maximum(l, 1e-30)


def _new_token_vis(rows, s_len):
    return _iota((rows, LANES), 1) <= (_iota((rows, LANES), 0) & (s_len - 1))


def _page_specs(n_pages, rows, layer):
    return [pl.BlockSpec((None, None, rows, LANES), functools.partial(lambda b, pt, p: (layer, pt[b, p], 0, 0), p=p))
            for p in range(n_pages)]


def _moba_sample_kernel(pt_ref, q_ref, kvn_ref, *rest, n_pages, s_len, page):
    pages, (o_ref, s_sc) = rest[:n_pages], rest[n_pages:]
    rows = HEADS * s_len
    q = _stack_heads(q_ref, LANES)
    qb = q.astype(BF16)
    ppb = MOBA_BLOCK // page
    n_blocks = n_pages // ppb
    lane = _iota((LANES, LANES), 1)
    k_mean_t = jnp.zeros((LANES, LANES), F32)
    for blk in range(n_blocks):
        tot = jnp.sum(pages[blk * ppb][0:LANES, :], axis=1, keepdims=True)
        for u in range(1, ppb):
            tot = tot + jnp.sum(pages[blk * ppb + u][0:LANES, :], axis=1, keepdims=True)
        k_mean_t = jnp.where(lane == blk, tot * (1.0 / MOBA_BLOCK), k_mean_t)
    blk_id = _iota((rows, LANES), 1)
    s_gate = jnp.where(blk_id < n_blocks, _mm(q, k_mean_t), NEG_INF)
    sel = jnp.where((blk_id < n_blocks) & (_rank_before(s_gate, n_blocks) < MOBA_TOPK), 1.0, 0.0)

    kv_new = _pad_rows(kvn_ref[...], LANES).astype(BF16)
    o = _paged_attend(
        n_pages, page, s_sc,
        lambda p: _mm(qb, pages[p][0:LANES, :]) * MOBA_SCALE,
        lambda p: jnp.broadcast_to(sel[:, p // ppb:p // ppb + 1], (rows, page)) > 0.5,
        lambda p: pages[p][LANES:2 * LANES, :],
        _mm_nt(qb, kv_new[:, 0:LANES]) * MOBA_SCALE, _new_token_vis(rows, s_len), kv_new[:, LANES:2 * LANES])
    _unstack_heads(o_ref, o, s_len)


def moba_sample(mq, mkv_new, cache_t, page_table, *, layer):
    b, s_len, _ = mq.shape
    n_pages = page_table.shape[1]
    page = cache_t.shape[3]
    assert (n_pages * page) % MOBA_BLOCK == 0 and MOBA_BLOCK % page == 0 and s_len <= LANES
    assert n_pages * page // MOBA_BLOCK >= MOBA_TOPK
    row = lambda w: pl.BlockSpec((None, s_len, w), lambda i, pt: (i, 0, 0))
    return pl.pallas_call(
        functools.partial(_moba_sample_kernel, n_pages=n_pages, s_len=s_len, page=page),
        grid_spec=pltpu.PrefetchScalarGridSpec(
            num_scalar_prefetch=1, grid=(b,),
            in_specs=[row(HEADS * LANES), row(256)] + _page_specs(n_pages, 256, layer),
            out_specs=row(HEADS * LANES),
            scratch_shapes=[pltpu.VMEM((HEADS * s_len, n_pages * page), F32)]),
        out_shape=jax.ShapeDtypeStruct((b, s_len, HEADS * LANES), F32),
        compiler_params=_params(1),
        name="moba_sample",
    )(page_table, mq, mkv_new, *([cache_t] * n_pages))


def _mla_sample_kernel(pt_ref, q_ref, kn_ref, *rest, n_pages, s_len, page):
    pages, (o_ref, s_sc) = rest[:n_pages], rest[n_pages:]
    rows = HEADS * s_len
    qb = _stack_heads(q_ref, MLA_QW).astype(BF16)
    q_lat, q_rope = qb[:, 0:MLA_KV_LORA], qb[:, MLA_KV_LORA:MLA_QW]
    k_new = _pad_rows(kn_ref[...], LANES).astype(BF16)

    def score(p):
        k_rope_t = _pad_rows(pages[p][MLA_KV_LORA:MLA_ROW, :], LANES)
        return (_mm(q_lat, pages[p][0:MLA_KV_LORA, :]) + _mm(q_rope, k_rope_t)) * MLA_SCALE

    o = _paged_attend(n_pages, page, s_sc, score, lambda p: None, lambda p: pages[p][0:MLA_KV_LORA, :],
                      _mm_nt(qb, k_new) * MLA_SCALE, _new_token_vis(rows, s_len), k_new[:, 0:MLA_KV_LORA])
    _unstack_heads(o_ref, o, s_len)


def mla_sample(qcat, kcat_new, cache_t, page_table, *, layer):
    b, s_len, _ = qcat.shape
    n_pages = page_table.shape[1]
    page = cache_t.shape[3]
    assert page == LANES and s_len <= LANES
    row = lambda w: pl.BlockSpec((None, s_len, w), lambda i, pt: (i, 0, 0))
    return pl.pallas_call(
        functools.partial(_mla_sample_kernel, n_pages=n_pages, s_len=s_len, page=page),
        grid_spec=pltpu.PrefetchScalarGridSpec(
            num_scalar_prefetch=1, grid=(b,),
            in_specs=[row(HEADS * MLA_QW), row(MLA_QW)] + _page_specs(n_pages, MLA_ROW, layer),
            out_specs=row(HEADS * LANES),
            scratch_shapes=[pltpu.VMEM((HEADS * s_len, n_pages * page), F32)]),
        out_shape=jax.ShapeDtypeStruct((b, s_len, HEADS * LANES), F32),
        compiler_params=_params(1),
        name="mla_sample",
    )(page_table, qcat, kcat_new, *([cache_t] * n_pages))


def _nsa_sample_kernel(pt_ref, qraw_ref, qrot_ref, gate_ref, slcn_ref, winn_ref, win_ref, pe_ref, w1_ref, w2_ref,
                       ovl_ref, *rest, n_pages, s_len, page, past_len):
    pages, (o_ref, xs_sc, s_sc) = rest[:n_pages], rest[n_pages:]
    rows = HEADS * s_len
    q_raw = _stack_heads(qraw_ref, LANES)
    q_rot = _stack_heads(qrot_ref, LANES).astype(BF16)

    for p in range(n_pages):
        xs_sc[p * page:(p + 1) * page, :] = pages[p][0:LANES, :].T
    n_chunks = past_len // NSA_CMP_STRIDE
    n_cmp = n_chunks - 1
    ckcv = _compress(lambda r: xs_sc[pl.ds(r, n_chunks, stride=NSA_CMP_STRIDE), :],
                     n_chunks, pe_ref, w1_ref, w2_ref).astype(BF16)
    q_pos_c = past_len + (_iota((rows, n_chunks), 0) & (s_len - 1))
    c_idx = _iota((rows, n_chunks), 1)
    vis_c = (c_idx * NSA_CMP_STRIDE + (NSA_CMP_LEN - 1) <= q_pos_c) & (c_idx < n_cmp)
    o_c, (p_c,) = _softmax_parts([(_mm_nt(q_raw, ckcv) * NSA_SCALE, vis_c, ckcv, False)])

    width = ovl_ref.shape[1]
    n_sel_blocks = -(-(past_len + s_len) // NSA_SEL_BLOCK)
    q_pos1 = past_len + _iota((s_len, width), 0)
    sel = jnp.where(_nsa_select(p_c, q_pos1, ovl_ref, s_len, width, n_sel_blocks), 1.0, 0.0)
    bpp = page // NSA_SEL_BLOCK
    in_page_blk = _iota((s_len, page), 1) >> SEL_SHIFT

    def vis_page(p):
        picked = jnp.zeros((s_len, page), F32)
        for u in range(bpp):
            col = jnp.broadcast_to(sel[:, p * bpp + u:p * bpp + u + 1], (s_len, page))
            picked = jnp.where(in_page_blk == u, col, picked)
        return _tile_rows(picked, HEADS) > 0.5

    new_blk = past_len // NSA_SEL_BLOCK
    vis_new = _new_token_vis(rows, s_len) & (
        _tile_rows(jnp.broadcast_to(sel[:, new_blk:new_blk + 1], (s_len, LANES)), HEADS) > 0.5)
    slc_new = _pad_rows(slcn_ref[...], LANES).astype(BF16)
    o_s = _paged_attend(
        n_pages, page, s_sc,
        lambda p: _mm(q_rot, pages[p][LANES:2 * LANES, :]) * NSA_SCALE, vis_page,
        lambda p: pages[p][LANES:2 * LANES, :],
        _mm_nt(q_rot, slc_new) * NSA_SCALE, vis_new, slc_new)

    wb = win_ref.shape[1]
    q_pos_w = past_len + (_iota((rows, wb), 0) & (s_len - 1))
    w_pos = past_len - wb + _iota((rows, wb), 1)
    dist = q_pos_w - w_pos
    vis_buf = (dist >= 0) & (dist <= NSA_WINDOW) & (w_pos >= 0)
    win_t = win_ref[...].astype(BF16)
    win_new = _pad_rows(winn_ref[...], LANES).astype(BF16)
    o_w, _ = _softmax_parts([(_mm(q_rot, win_t) * NSA_SCALE, vis_buf, win_t, True),
                             (_mm_nt(q_rot, win_new) * NSA_SCALE, _new_token_vis(rows, s_len), win_new, False)])

    gate = gate_ref[...]
    for h in range(HEADS):
        sl = slice(h * s_len, (h + 1) * s_len)
        o_ref[:, h * LANES:(h + 1) * LANES] = (gate[:, h:h + 1] * o_c[sl]
                                               + gate[:, HEADS + h:HEADS + h + 1] * o_s[sl]
                                               + gate[:, 2 * HEADS + h:2 * HEADS + h + 1] * o_w[sl])


def nsa_sample(nq_raw, nq_rot, gate, slc_new, win_new, cache_t, win_t, page_table, pe, w1, w2, *, layer):
    b, s_len, _ = nq_raw.shape
    n_pages = page_table.shape[1]
    page = cache_t.shape[3]
    past_len = n_pages * page
    wb = win_t.shape[3]
    assert page == LANES and page % NSA_SEL_BLOCK == 0 and past_len % NSA_CMP_STRIDE == 0
    assert s_len < NSA_CMP_STRIDE and s_len <= NSA_SEL_BLOCK and wb <= past_len and NSA_WINDOW <= wb + s_len
    n_chunks = past_len // NSA_CMP_STRIDE
    n_sel_blocks = -(-(past_len + s_len) // NSA_SEL_BLOCK)
    width = -(-n_sel_blocks // LANES) * LANES
    ovl = _overlap_matrix(n_chunks, n_chunks - 1, width)
    row = lambda w: pl.BlockSpec((None, s_len, w), lambda i, pt: (i, 0, 0))
    const = lambda a: pl.BlockSpec(a.shape, lambda i, pt: (0,) * a.ndim)
    return pl.pallas_call(
        functools.partial(_nsa_sample_kernel, n_pages=n_pages, s_len=s_len, page=page, past_len=past_len),
        grid_spec=pltpu.PrefetchScalarGridSpec(
            num_scalar_prefetch=1, grid=(b,),
            in_specs=[row(HEADS * LANES), row(HEADS * LANES), row(LANES), row(LANES), row(LANES),
                      pl.BlockSpec((None, None, LANES, wb), lambda i, pt: (layer, i, 0, 0)),
                      const(pe), const(w1), const(w2), const(ovl)] + _page_specs(n_pages, 256, layer),
            out_specs=row(HEADS * LANES),
            scratch_shapes=[pltpu.VMEM((past_len, LANES), F32), pltpu.VMEM((HEADS * s_len, past_len), F32)]),
        out_shape=jax.ShapeDtypeStruct((b, s_len, HEADS * LANES), F32),
        compiler_params=_params(1),
        name="nsa_sample",
    )(page_table, nq_raw, nq_rot, gate, slc_new, win_new, win_t, pe, w1, w2, ovl, *([cache_t] * n_pages))


def _partner(d, rot):
    half = rot // 2
    return d + half if d < half else (d - half if d < rot else -1)


def _w_all_columns(q_lane_offset, kv_group_major):
    src = np.full(W_ALL_COLS, -1, np.int64)
    for h in range(HEADS):
        for d in range(DH):
            pd = _partner(d, ROT16)
            mq = h * LANES + q_lane_offset(h) + d
            src[G_MQ + mq] = S_MQ + h * DH + d
            src[G_NQ + h * LANES + d] = S_NQ + h * DH + d
            if pd >= 0:
                src[G_MQR + mq] = S_MQ + h * DH + pd
                src[G_NQR + h * LANES + d] = S_NQ + h * DH + pd
    for g in range(MOBA_KV_HEADS):
        for d in range(DH):
            k_col = g * LANES + d if kv_group_major else g * DH + d
            v_col = g * LANES + DH + d if kv_group_major else LANES + g * DH + d
            src[G_MKV + k_col] = S_MK + g * DH + d
            src[G_MKV + v_col] = S_MV + g * DH + d
            if _partner(d, ROT16) >= 0:
                src[G_MKVR + k_col] = S_MK + g * DH + _partner(d, ROT16)
    src[G_CQ:G_CQ + MLA_Q_LORA] = S_CQ + np.arange(MLA_Q_LORA)
    src[G_CKV:G_CKV + MLA_KV_LORA] = S_CKV + np.arange(MLA_KV_LORA)
    for r in range(MLA_ROPE):
        src[G_KR + r] = S_KR + r
        src[G_KRR + r] = S_KR + _partner(r, MLA_ROPE)
    src[G_NSA:G_NSA + 4 * DH] = S_KC + np.arange(4 * DH)
    src[G_WIN:G_WIN + 2 * DH] = S_KW + np.arange(2 * DH)
    for d in range(ROT16):
        src[G_NSAR + d] = S_KS + _partner(d, ROT16)
        src[G_WINR + d] = S_KW + _partner(d, ROT16)
    src[G_NG:G_NG + 3 * HEADS] = S_NG + np.arange(3 * HEADS)
    return src


def _take_cols(w, src):
    picked = jnp.take(w, jnp.asarray(np.maximum(src, 0)), axis=-1)
    return jnp.where(jnp.asarray(src >= 0), picked, 0.0)


def _pad_head_rows(w, lane_offset):
    src = np.full(HEADS * LANES, -1, np.int64)
    for h in range(HEADS):
        src[h * LANES + lane_offset(h) + np.arange(DH)] = h * DH + np.arange(DH)
    picked = jnp.take(w, jnp.asarray(np.maximum(src, 0)), axis=1)
    return jnp.where(jnp.asarray(src >= 0)[None, :, None], picked, 0.0)


def _block_diag_heads(w):
    l, h, a, b = w.shape
    eye = jnp.eye(h, dtype=w.dtype)
    return (w[:, :, :, None, :] * eye[None, :, None, :, None]).reshape(l, h * a, h * b)


def _mla_query_weights(w_uq, w_uk):
    src = np.zeros(1024, np.int64)
    per = MLA_NOPE + MLA_ROPE
    for h in range(HEADS):
        src[h * MLA_NOPE + np.arange(MLA_NOPE)] = h * per + np.arange(MLA_NOPE)
        for r in range(MLA_ROPE):
            src[512 + h * MLA_ROPE + r] = h * per + MLA_NOPE + r
            src[768 + h * MLA_ROPE + r] = h * per + MLA_NOPE + _partner(r, MLA_ROPE)
    w_uq_all = _take_cols(w_uq, src)
    depth = w_uk.shape[0]
    wk = w_uk.reshape(depth, MLA_KV_LORA, HEADS, MLA_NOPE).transpose(0, 2, 3, 1)
    wk = jnp.pad(wk, ((0, 0), (0, 0), (0, 0), (0, MLA_QW - MLA_KV_LORA)))
    top = _block_diag_heads(wk)
    bottom = np.zeros((HEADS * MLA_ROPE, HEADS * MLA_QW), np.float32)
    for h in range(HEADS):
        for r in range(MLA_ROPE):
            bottom[h * MLA_ROPE + r, h * MLA_QW + MLA_KV_LORA + r] = 1.0
    w_cat = jnp.concatenate([top, jnp.broadcast_to(jnp.asarray(bottom), (depth,) + bottom.shape)], axis=1)
    return w_uq_all.astype(BF16), w_cat.astype(BF16)


def _compress_weights(pe_k, w1_k, w2_k, pe_v, w1_v, w2_v):
    depth = pe_k.shape[0]
    pe = jnp.concatenate([pe_k, pe_v], axis=-1)
    w1 = jnp.stack([w1_k.reshape(depth, NSA_CMP_LEN, DH, NSA_PHI_HIDDEN),
                    w1_v.reshape(depth, NSA_CMP_LEN, DH, NSA_PHI_HIDDEN)], axis=2)
    w1 = _block_diag_heads(w1.reshape(depth * NSA_CMP_LEN, 2, DH, NSA_PHI_HIDDEN))
    w1 = w1.reshape(depth, NSA_CMP_LEN, 2 * DH, 2 * NSA_PHI_HIDDEN)
    w2 = _block_diag_heads(jnp.stack([w2_k, w2_v], axis=1))
    return pe, w1.astype(BF16), w2.astype(BF16)


def _rope_tables(pos):
    posf = pos.astype(F32)[:, None]
    lane = np.arange(LANES)

    def tables(width, rot):
        d = lane % width
        half = rot // 2
        inv_freq = ROPE_THETA ** (-2.0 * jnp.arange(half, dtype=F32) / rot)
        ang = posf * inv_freq[jnp.asarray(d % half)]
        cos, sin = jnp.cos(ang), jnp.sin(ang)
        c = jnp.where(jnp.asarray(d < rot), cos, 1.0)
        s = jnp.where(jnp.asarray(d < half), -sin, jnp.where(jnp.asarray(d < rot), sin, 0.0))
        return c, s

    ca, sa = tables(DH, ROT16)
    cb, sb = tables(MLA_ROPE, MLA_ROPE)
    return ca, sa, cb, sb


def _forward(cfg, x_prompt, x_sample, cache_moba, cache_mla, cache_nsa, state_nsa_win, page_table,
             norm_mix, w_in, mla_q_norm, mla_w_uq, mla_kv_norm, mla_w_uk, mla_w_uv,
             nsa_pe_k, nsa_w1_k, nsa_w2_k, nsa_pe_v, nsa_w1_v, nsa_w2_v,
             w_oa, w_ob, w_oc, w_gate, w_out, norm_ffn, w_ff1, w_ff2, norm_final):
    n, t, b, s_len = cfg.n_seq, cfg.t_len, cfg.dec_batch, cfg.dec_seq
    depth = w_in.shape[0]
    mp, ms = n * t, b * s_len
    tm_p, tm_s = 128, min(128, ms)
    bf = lambda w: w.astype(BF16)
    vec = lambda g: g.reshape(depth, 1, -1)

    value_half = lambda h: DH
    group_lane = lambda h: DH * (h // MOBA_HPG)
    w_all_p = bf(_take_cols(w_in, _w_all_columns(lambda h: 0, True)))
    w_all_s = bf(_take_cols(w_in, _w_all_columns(group_lane, False)))
    w_uq_all, w_cat = _mla_query_weights(mla_w_uq, mla_w_uk)
    w_uv_bd = bf(_block_diag_heads(mla_w_uv.reshape(depth, MLA_KV_LORA, HEADS, DH).transpose(0, 2, 1, 3)))
    w_oa_p, w_oa_s = bf(_pad_head_rows(w_oa, value_half)), bf(_pad_head_rows(w_oa, group_lane))
    w_oc_pad = bf(_pad_head_rows(w_oc, value_half))
    pe, w1c, w2c = _compress_weights(nsa_pe_k, nsa_w1_k, nsa_w2_k, nsa_pe_v, nsa_w1_v, nsa_w2_v)
    w_ob_b, w_gate_b, w_out_b, w_ff1_b, w_ff2_b = bf(w_ob), bf(w_gate), bf(w_out), bf(w_ff1), bf(w_ff2)
    g_mix, g_ffn, g_q, g_kv = vec(norm_mix), vec(norm_ffn), vec(mla_q_norm), vec(mla_kv_norm)
    g_final = norm_final.reshape(1, -1)

    tabs_p = _rope_tables(jnp.arange(t))
    tabs_s = _rope_tables(cfg.past_len + (jnp.arange(tm_s) % s_len))

    pool = cache_moba.shape[1]
    cm_t = cache_moba.transpose(0, 1, 3, 4, 5, 2).reshape(depth, pool, 4 * DH, cfg.page)
    cn_t = cache_nsa.transpose(0, 1, 3, 4, 2).reshape(depth, pool, 4 * DH, cfg.page)
    cl_t = cache_mla.transpose(0, 1, 3, 2)
    win_t = state_nsa_win.transpose(0, 1, 3, 4, 2).reshape(depth, b, 2 * DH, -1)

    xp = x_prompt.reshape(mp, D_MODEL)
    xs = x_sample.reshape(ms, D_MODEL)
    leaves = [[] for _ in range(8)]
    for l in range(depth):
        last = l == depth - 1
        mq, mkv, qcat, kcat, nq_raw, nq_rot, cmp_r, slc_r, win_r, gate = in_proj(
            xp, g_mix[l], w_all_p[l], g_q[l], w_uq_all[l], w_cat[l], g_kv[l], tabs_p, tm=tm_p, kv_group_major=True)
        r3 = lambda a: a.reshape(n, t, -1)
        o_a = moba_prompt(r3(mq), r3(mkv))
        o_b = mla_prompt(r3(qcat), r3(kcat))
        ckcv = nsa_compress_prompt(r3(cmp_r), pe[l], w1c[l], w2c[l])
        o_c = nsa_prompt(r3(nq_raw), r3(nq_rot), r3(gate), ckcv, r3(slc_r), r3(win_r))
        x1 = merge(xp, g_mix[l], w_gate_b[l], o_a.reshape(mp, -1), o_b.reshape(mp, -1), o_c.reshape(mp, -1),
                   w_uv_bd[l], w_oa_p[l], w_ob_b[l], w_oc_pad[l], w_out_b[l], tm=256)
        xp = ffn(x1, g_ffn[l], w_ff1_b[l], w_ff2_b[l], g_final, tm=256, final_norm=last)
        leaves[0].append(mkv.reshape(n, t, MOBA_KV_HEADS, 2, DH).transpose(0, 1, 3, 2, 4))
        leaves[2].append(kcat[:, :MLA_ROW].reshape(n, t, MLA_ROW))
        leaves[4].append(jnp.concatenate([cmp_r, slc_r], axis=1).reshape(n, t, 4, DH))
        leaves[6].append(win_r.reshape(n, t, 2, DH)[:, t - min(NSA_WINDOW, t):])
        mq, mkv, qcat, kcat, nq_raw, nq_rot, cmp_r, slc_r, win_r, gate = in_proj(
            xs, g_mix[l], w_all_s[l], g_q[l], w_uq_all[l], w_cat[l], g_kv[l], tabs_s, tm=tm_s, kv_group_major=False)
        r3 = lambda a: a.reshape(b, s_len, -1)
        o_a = moba_sample(r3(mq), r3(mkv), cm_t, page_table, layer=l)
        o_b = mla_sample(r3(qcat), r3(kcat), cl_t, page_table, layer=l)
        o_c = nsa_sample(r3(nq_raw), r3(nq_rot), r3(gate), r3(slc_r), r3(win_r), cn_t, win_t, page_table,
                         pe[l], w1c[l], w2c[l], layer=l)
        x1 = merge(xs, g_mix[l], w_gate_b[l], o_a.reshape(ms, -1), o_b.reshape(ms, -1), o_c.reshape(ms, -1),
                   w_uv_bd[l], w_oa_s[l], w_ob_b[l], w_oc_pad[l], w_out_b[l], tm=min(256, ms))
        xs = ffn(x1, g_ffn[l], w_ff1_b[l], w_ff2_b[l], g_final, tm=min(256, ms), final_norm=last)
        leaves[1].append(mkv.reshape(b, s_len, 2, MOBA_KV_HEADS, DH))
        leaves[3].append(kcat[:, :MLA_ROW].reshape(b, s_len, MLA_ROW))
        leaves[5].append(jnp.concatenate([cmp_r, slc_r], axis=1).reshape(b, s_len, 4, DH))
        win_full = jnp.concatenate([state_nsa_win[l], win_r.reshape(b, s_len, 2, DH)], axis=1)
        leaves[7].append(win_full[:, win_full.shape[1] - min(NSA_WINDOW, win_full.shape[1]):])
    return (xp.reshape(n, t, D_MODEL), xs.reshape(b, s_len, D_MODEL)) + tuple(jnp.stack(v) for v in leaves)


def kernel(x_prompt, x_sample, cache_moba, cache_mla, cache_nsa, state_nsa_win, page_table, norm_mix, w_in,
           mla_q_norm, mla_w_uq, mla_kv_norm, mla_w_uk, mla_w_uv, nsa_pe_k, nsa_w1_k, nsa_w2_k, nsa_pe_v,
           nsa_w1_v, nsa_w2_v, w_oa, w_ob, w_oc, w_gate, w_out, norm_ffn, w_ff1, w_ff2, norm_final):
    return _forward(PROD, x_prompt, x_sample, cache_moba, cache_mla, cache_nsa, state_nsa_win, page_table,
                    norm_mix, w_in, mla_q_norm, mla_w_uq, mla_kv_norm, mla_w_uk, mla_w_uv,
                    nsa_pe_k, nsa_w1_k, nsa_w2_k, nsa_pe_v, nsa_w1_v, nsa_w2_v,
                    w_oa, w_ob, w_oc, w_gate, w_out, norm_ffn, w_ff1, w_ff2, norm_final)
```

```python
import collections
import functools

import numpy as np
import jax
import jax.numpy as jnp
from jax import lax
from jax.experimental import pallas as pl
from jax.experimental.pallas import tpu as pltpu

F32 = jnp.float32
BF16 = jnp.bfloat16

D_MODEL = 1024
BATCH = 4
SEQ = 4096
DEPTH = 4
DEC_BATCH = 128
DEC_SEQ = 8
PAST_LEN = 8192
PAGE_SIZE = 128

ROPE_THETA = 500000.0
NORM_EPS = 1e-6
NEG_INF = -1e30
SEL_FORCE = 1e30

HEADS = 8
DH = 64
LANES = 128
ROW_CHUNK = 128
MOBA_KV_HEADS = 2
MOBA_HPG = HEADS // MOBA_KV_HEADS
MOBA_BLOCK = 256
MOBA_TOPK = 3
MOBA_SCALE = DH ** -0.5
ROT16 = 16

MLA_NOPE = 64
MLA_ROPE = 32
MLA_Q_LORA = 256
MLA_KV_LORA = 128
MLA_ROW = MLA_KV_LORA + MLA_ROPE
MLA_QW = 256
MLA_SCALE = (MLA_NOPE + MLA_ROPE) ** -0.5

NSA_CMP_LEN = 32
NSA_CMP_STRIDE = 16
NSA_SEL_BLOCK = 64
SEL_SHIFT = 6
NSA_SEL_TOPN = 16
NSA_WINDOW = 512
NSA_PHI_HIDDEN = 128
NSA_SCALE = DH ** -0.5

D_FF = 4 * D_MODEL

IN_WIDTHS = (HEADS * DH, MOBA_KV_HEADS * DH, MOBA_KV_HEADS * DH, MLA_Q_LORA, MLA_KV_LORA, MLA_ROPE,
             HEADS * DH, DH, DH, DH, DH, DH, DH, 3 * HEADS)
(S_MQ, S_MK, S_MV, S_CQ, S_CKV, S_KR, S_NQ, S_KC, S_VC, S_KS, S_VS, S_KW, S_VW, S_NG) = (
    int(v) for v in np.concatenate([[0], np.cumsum(IN_WIDTHS)[:-1]]))

G_MQ, G_MQR = 0, 1024
G_MKV, G_MKVR = 2048, 2304
G_CQ, G_CKV = 2560, 2816
G_KR, G_KRR = 2944, 3072
G_NQ, G_NQR = 3200, 4224
G_NSA, G_NSAR = 5248, 5504
G_WIN, G_WINR = 5760, 5888
G_NG = 6016
W_ALL_COLS = 6144

VMEM_LIMIT = 56 * 1024 * 1024

Cfg = collections.namedtuple("Cfg", "n_seq t_len dec_batch dec_seq past_len page")
PROD = Cfg(BATCH, SEQ, DEC_BATCH, DEC_SEQ, PAST_LEN, PAGE_SIZE)


def _params(n_axes):
    return pltpu.CompilerParams(dimension_semantics=("arbitrary",) * n_axes, vmem_limit_bytes=VMEM_LIMIT)


def _rms(x, g):
    return x * lax.rsqrt(jnp.mean(x * x, axis=-1, keepdims=True) + NORM_EPS) * g


def _mm(a, b):
    return jnp.dot(a.astype(BF16), b.astype(BF16), preferred_element_type=F32)


def _mm_nt(a, b):
    return lax.dot_general(a.astype(BF16), b.astype(BF16), (((1,), (1,)), ((), ())),
                           preferred_element_type=F32)


def _sigmoid(x):
    return 1.0 / (1.0 + jnp.exp(-x))


def _tile_lanes(t, n):
    return t if n == 1 else jnp.concatenate([t] * n, axis=1)


def _tile_rows(t, n):
    return t if n == 1 else jnp.concatenate([t] * n, axis=0)


def _iota(shape, dim):
    return lax.broadcasted_iota(jnp.int32, shape, dim)


def _const_spec(shape):
    nd = len(shape)
    return pl.BlockSpec(shape, lambda *_: (0,) * nd)


def _in_proj_kernel(x_ref, g_ref, wall_ref, qn_ref, wuq_ref, wcat_ref, kvn_ref,
                    ca_ref, sa_ref, cb_ref, sb_ref,
                    mq_ref, mkv_ref, qcat_ref, kcat_ref, nqraw_ref, nqrot_ref, cmp_ref, slc_ref, win_ref, gate_ref,
                    *, kv_group_major):
    hb = _rms(x_ref[...], g_ref[...]).astype(BF16)

    def proj(lo, width):
        return jnp.dot(hb, wall_ref[:, lo:lo + width], preferred_element_type=F32)

    ca, sa, cb, sb = ca_ref[...], sa_ref[...], cb_ref[...], sb_ref[...]
    rows = hb.shape[0]

    def rope16(z, zr, n_blocks, key_lanes=None):
        out = z * _tile_lanes(ca, n_blocks) + zr * _tile_lanes(sa, n_blocks)
        if key_lanes is None:
            return out
        return jnp.where(key_lanes(_iota((rows, n_blocks * LANES), 1)), out, z)

    mq_ref[...] = rope16(proj(G_MQ, 1024), proj(G_MQR, 1024), 8)
    if kv_group_major:
        k_lanes = lambda lane: (lane & (LANES - 1)) < DH
    else:
        k_lanes = lambda lane: lane < LANES
    mkv_ref[...] = rope16(proj(G_MKV, 256), proj(G_MKVR, 256), 2, k_lanes)

    cqn = _rms(proj(G_CQ, MLA_Q_LORA), qn_ref[...]).astype(BF16)
    qall = jnp.dot(cqn, wuq_ref[...], preferred_element_type=F32)
    q_rope = qall[:, 512:768] * _tile_lanes(cb, 2) + qall[:, 768:1024] * _tile_lanes(sb, 2)
    qin = jnp.concatenate([qall[:, :512], q_rope], axis=1).astype(BF16)
    qcat_ref[...] = jnp.dot(qin, wcat_ref[...], preferred_element_type=F32)
    kcat_ref[:, 0:LANES] = _rms(proj(G_CKV, MLA_KV_LORA), kvn_ref[...])
    kcat_ref[:, LANES:2 * LANES] = proj(G_KR, LANES) * cb + proj(G_KRR, LANES) * sb

    nq = proj(G_NQ, 1024)
    nqraw_ref[...] = nq
    nqrot_ref[...] = rope16(nq, proj(G_NQR, 1024), 8)
    key_half = lambda lane: lane < DH
    cmp_ref[...] = proj(G_NSA, LANES)
    slc_ref[...] = rope16(proj(G_NSA + LANES, LANES), proj(G_NSAR, LANES), 1, key_half)
    win_ref[...] = rope16(proj(G_WIN, LANES), proj(G_WINR, LANES), 1, key_half)
    gate_ref[...] = _sigmoid(proj(G_NG, LANES))


def in_proj(x, g_mix, w_all, q_norm, w_uq_all, w_cat, kv_norm, tabs, *, tm, kv_group_major):
    m = x.shape[0]
    ca, sa, cb, sb = tabs
    n_per = ca.shape[0] // tm
    row = lambda w: pl.BlockSpec((tm, w), lambda i: (i, 0))
    tab = pl.BlockSpec((tm, LANES), lambda i: (i % n_per, 0))
    widths = (1024, 256, HEADS * MLA_QW, 256, 1024, 1024, LANES, LANES, LANES, LANES)
    return pl.pallas_call(
        functools.partial(_in_proj_kernel, kv_group_major=kv_group_major),
        grid=(m // tm,),
        in_specs=[row(D_MODEL), _const_spec((1, D_MODEL)), _const_spec((D_MODEL, W_ALL_COLS)),
                  _const_spec((1, MLA_Q_LORA)), _const_spec((MLA_Q_LORA, 1024)),
                  _const_spec((768, HEADS * MLA_QW)), _const_spec((1, MLA_KV_LORA)), tab, tab, tab, tab],
        out_specs=[row(w) for w in widths],
        out_shape=[jax.ShapeDtypeStruct((m, w), F32) for w in widths],
        compiler_params=_params(1),
        name="in_proj",
    )(x, g_mix, w_all, q_norm, w_uq_all, w_cat, kv_norm, ca, sa, cb, sb)


def _merge_kernel(x_ref, g_ref, wgate_ref, oa_ref, ob_ref, oc_ref, wuv_ref, woa_ref, wob_ref, woc_ref,
                  wout_ref, out_ref):
    x = x_ref[...]
    hb = _rms(x, g_ref[...]).astype(BF16)

    def gate(k):
        return _sigmoid(jnp.dot(hb, wgate_ref[:, k * D_MODEL:(k + 1) * D_MODEL], preferred_element_type=F32))

    o_b = _mm(ob_ref[...], wuv_ref[...])
    merged = gate(0) * _mm(oa_ref[...], woa_ref[...])
    merged = merged + gate(1) * _mm(o_b, wob_ref[...])
    merged = merged + gate(2) * _mm(oc_ref[...], woc_ref[...])
    out_ref[...] = x + _mm(merged, wout_ref[...])


def merge(x, g_mix, w_gate, o_a, o_b_lat, o_c, w_uv_bd, w_oa, w_ob, w_oc, w_out, *, tm):
    m = x.shape[0]
    row = lambda w: pl.BlockSpec((tm, w), lambda i: (i, 0))
    return pl.pallas_call(
        _merge_kernel,
        grid=(m // tm,),
        in_specs=[row(D_MODEL), _const_spec((1, D_MODEL)), _const_spec((D_MODEL, 3 * D_MODEL)),
                  row(1024), row(1024), row(1024), _const_spec((1024, 512)),
                  _const_spec((1024, D_MODEL)), _const_spec((512, D_MODEL)), _const_spec((1024, D_MODEL)),
                  _const_spec((D_MODEL, D_MODEL))],
        out_specs=row(D_MODEL),
        out_shape=jax.ShapeDtypeStruct((m, D_MODEL), F32),
        compiler_params=_params(1),
        name="merge",
    )(x, g_mix, w_gate, o_a, o_b_lat, o_c, w_uv_bd, w_oa, w_ob, w_oc, w_out)


def _ffn_kernel(x_ref, g_ref, w1_ref, w2_ref, gf_ref, out_ref, *, final_norm):
    x = x_ref[...]
    hb = _rms(x, g_ref[...]).astype(BF16)
    acc = x
    for c in range(D_FF // D_MODEL):
        u = jnp.maximum(jnp.dot(hb, w1_ref[:, c * D_MODEL:(c + 1) * D_MODEL], preferred_element_type=F32), 0.0)
        acc = acc + jnp.dot((u * u).astype(BF16), w2_ref[c * D_MODEL:(c + 1) * D_MODEL, :],
                            preferred_element_type=F32)
    out_ref[...] = _rms(acc, gf_ref[...]) if final_norm else acc


def ffn(x, g_ffn, w1, w2, g_final, *, tm, final_norm):
    m = x.shape[0]
    row = pl.BlockSpec((tm, D_MODEL), lambda i: (i, 0))
    return pl.pallas_call(
        functools.partial(_ffn_kernel, final_norm=final_norm),
        grid=(m // tm,),
        in_specs=[row, _const_spec((1, D_MODEL)), _const_spec((D_MODEL, D_FF)), _const_spec((D_FF, D_MODEL)),
                  _const_spec((1, D_MODEL))],
        out_specs=row,
        out_shape=jax.ShapeDtypeStruct((m, D_MODEL), F32),
        compiler_params=_params(1),
        name="ffn",
    )(x, g_ffn, w1, w2, g_final)


def _stack_heads(ref, width):
    return jnp.concatenate([ref[:, h * width:(h + 1) * width] for h in range(HEADS)], axis=0)


def _unstack_heads(out_ref, o, rows):
    for h in range(HEADS):
        out_ref[:, h * LANES:(h + 1) * LANES] = o[h * rows:(h + 1) * rows, :]


def _softmax_parts(parts):
    m = None
    for s, mask, _, _ in parts:
        mt = jnp.max(jnp.where(mask, s, NEG_INF), axis=1, keepdims=True)
        m = mt if m is None else jnp.maximum(m, mt)
    l, es = None, []
    for s, mask, _, _ in parts:
        e = jnp.where(mask, jnp.exp(s - m), 0.0)
        es.append(e)
        lt = jnp.sum(e, axis=1, keepdims=True)
        l = lt if l is None else l + lt
    inv = 1.0 / jnp.maximum(l, 1e-30)
    acc, ps = None, []
    for e, (_, _, v, v_t) in zip(es, parts):
        p = e * inv
        ps.append(p)
        o = _mm_nt(p, v) if v_t else _mm(p, v)
        acc = o if acc is None else acc + o
    return acc, ps


def _attend_once(q, k_op, v_op, vis):
    s_all = _mm_nt(q, k_op)
    ps = []
    for c in range(q.shape[0] // ROW_CHUNK):
        s = s_all[c * ROW_CHUNK:(c + 1) * ROW_CHUNK]
        m = jnp.max(jnp.where(vis, s, NEG_INF), axis=1, keepdims=True)
        e = jnp.where(vis, jnp.exp(s - m), 0.0)
        ps.append(e * (1.0 / jnp.maximum(jnp.sum(e, axis=1, keepdims=True), 1e-30)))
    return _mm(jnp.concatenate([p.astype(BF16) for p in ps], axis=0), v_op), ps


def _online_init(m_sc, l_sc, acc_sc):
    m_sc[...] = jnp.full(m_sc.shape, NEG_INF, F32)
    l_sc[...] = jnp.zeros(l_sc.shape, F32)
    acc_sc[...] = jnp.zeros(acc_sc.shape, F32)


def _flash_update(m_sc, l_sc, acc_sc, q, k_op, v_op, vis=None):
    m_all, l_all = m_sc[...], l_sc[...]
    s_all = _mm_nt(q, k_op)
    reps = k_op.shape[0] // LANES
    m_out, l_out, alphas, es = [], [], [], []
    for c in range(q.shape[0] // ROW_CHUNK):
        r = slice(c * ROW_CHUNK, (c + 1) * ROW_CHUNK)
        s = s_all[r]
        if vis is not None:
            s = jnp.where(vis(c), s, NEG_INF)
        m_old = m_all[r]
        m_new = jnp.maximum(m_old, jnp.max(s, axis=1, keepdims=True))
        e = jnp.exp(s - _tile_lanes(m_new, reps))
        alpha = jnp.exp(m_old - m_new)
        m_out.append(m_new)
        l_out.append(alpha * l_all[r] + jnp.sum(e, axis=1, keepdims=True))
        alphas.append(alpha)
        es.append(e.astype(BF16))
    m_sc[...] = jnp.concatenate(m_out, axis=0)
    l_sc[...] = jnp.concatenate(l_out, axis=0)
    acc_sc[...] = jnp.concatenate(alphas, axis=0) * acc_sc[...] + _mm(jnp.concatenate(es, axis=0), v_op)


def _block_bias_keys(kv, key_block):
    lane = _iota(kv.shape, 1)
    return jnp.where(lane < DH, kv, jnp.where(lane - DH == key_block, NEG_INF, 0.0)).astype(BF16)


def _online_finish(l_sc, acc_sc):
    return acc_sc[...] / jnp.maximum(l_sc[...], 1e-30)


def _rank_before(score, n_cols):
    lane = _iota(score.shape, 1)
    cnt = jnp.zeros(score.shape, F32)
    for j in range(n_cols):
        col = score[:, j:j + 1]
        ahead = (col > score) | ((col == score) & (lane > j))
        cnt = cnt + jnp.where(ahead, 1.0, 0.0)
    return cnt


def _mla_prompt_kernel(q_ref, k_ref, o_ref, m_sc, l_sc, acc_sc, *, tq, tk):
    i = pl.program_id(1)
    q = (_stack_heads(q_ref, MLA_QW) * MLA_SCALE).astype(BF16)
    _online_init(m_sc, l_sc, acc_sc)

    def tile(j, vis):
        kt = k_ref[pl.ds(pl.multiple_of(j * tk, tk), tk), :].astype(BF16)
        _flash_update(m_sc, l_sc, acc_sc, q, kt, kt[:, :MLA_KV_LORA], vis)

    diag = (i * tq) // tk
    causal = diag * tk + _iota((tq, tk), 1) <= i * tq + _iota((tq, tk), 0)
    tile(diag, lambda c: causal)
    lax.fori_loop(0, diag, lambda j, carry: tile(j, None) or carry, 0)
    _unstack_heads(o_ref, _online_finish(l_sc, acc_sc), tq)


def mla_prompt(qcat, kcat, *, tq=ROW_CHUNK, tk=512):
    n, t, _ = qcat.shape
    tk = min(tk, t)
    assert tq == ROW_CHUNK and tk % tq == 0
    rows = HEADS * tq
    return pl.pallas_call(
        functools.partial(_mla_prompt_kernel, tq=tq, tk=tk),
        grid=(n, t // tq),
        in_specs=[pl.BlockSpec((None, tq, HEADS * MLA_QW), lambda b, i: (b, i, 0)),
                  pl.BlockSpec((None, t, MLA_QW), lambda b, i: (b, 0, 0))],
        out_specs=pl.BlockSpec((None, tq, HEADS * LANES), lambda b, i: (b, i, 0)),
        out_shape=jax.ShapeDtypeStruct((n, t, HEADS * LANES), F32),
        scratch_shapes=[pltpu.VMEM((rows, LANES), F32), pltpu.VMEM((rows, LANES), F32),
                        pltpu.VMEM((rows, MLA_KV_LORA), F32)],
        compiler_params=_params(2),
        name="mla_prompt",
    )(qcat, kcat)


def _moba_prompt_kernel(q_ref, kv_ref, o_ref, m_sc, l_sc, acc_sc, *, n_blocks):
    i = pl.program_id(2)
    tq = MOBA_BLOCK
    rows = MOBA_HPG * tq
    q = jnp.concatenate([q_ref[:, h * LANES:(h + 1) * LANES] for h in range(MOBA_HPG)], axis=0)
    nb_pad = -(-n_blocks // 8) * 8
    k_mean = jnp.concatenate(
        [jnp.mean(kv_ref[j * tq:(j + 1) * tq, :], axis=0, keepdims=True) for j in range(n_blocks)]
        + ([jnp.zeros((nb_pad - n_blocks, LANES), F32)] if nb_pad > n_blocks else []), axis=0)
    blk = _iota((nb_pad, rows), 0)
    past = blk < i
    s_gate = jnp.where(past, _mm_nt(k_mean, q), NEG_INF)
    cnt = jnp.zeros((nb_pad, rows), F32)
    for j in range(n_blocks):
        other = s_gate[j:j + 1, :]
        cnt = cnt + jnp.where((other > s_gate) | ((other == s_gate) & (blk > j)), 1.0, 0.0)
    not_sel = jnp.where((blk < n_blocks) & ~(past & (cnt < MOBA_TOPK)), 1.0, 0.0)
    flags = jnp.concatenate([jnp.zeros((DH, rows), F32), not_sel, jnp.zeros((LANES - DH - nb_pad, rows), F32)],
                            axis=0).T
    qs = q * MOBA_SCALE
    qb = qs.astype(BF16)
    q_aug = (qs + flags).astype(BF16)

    _online_init(m_sc, l_sc, acc_sc)
    chunks_per_head = tq // ROW_CHUNK
    q_in = _iota((ROW_CHUNK, tq), 0)
    col = _iota((ROW_CHUNK, tq), 1)
    kv_own = kv_ref[pl.ds(pl.multiple_of(i * tq, tq), tq), :].astype(BF16)
    _flash_update(m_sc, l_sc, acc_sc, qb, kv_own, kv_own,
                  lambda c: col <= q_in + (c % chunks_per_head) * ROW_CHUNK)

    def past_block(j, carry):
        kv_j = kv_ref[pl.ds(pl.multiple_of(j * tq, tq), tq), :]
        _flash_update(m_sc, l_sc, acc_sc, q_aug, _block_bias_keys(kv_j, j), kv_j.astype(BF16))
        return carry

    lax.fori_loop(0, i, past_block, 0)
    o = _online_finish(l_sc, acc_sc)
    for h in range(MOBA_HPG):
        o_ref[:, h * LANES:(h + 1) * LANES] = o[h * tq:(h + 1) * tq, :]


def moba_prompt(mq, mkv):
    n, t, _ = mq.shape
    n_blocks = t // MOBA_BLOCK
    rows = MOBA_HPG * MOBA_BLOCK
    gw = MOBA_HPG * LANES
    return pl.pallas_call(
        functools.partial(_moba_prompt_kernel, n_blocks=n_blocks),
        grid=(n, MOBA_KV_HEADS, n_blocks),
        in_specs=[pl.BlockSpec((None, MOBA_BLOCK, gw), lambda b, g, i: (b, i, g)),
                  pl.BlockSpec((None, t, LANES), lambda b, g, i: (b, 0, g))],
        out_specs=pl.BlockSpec((None, MOBA_BLOCK, gw), lambda b, g, i: (b, i, g)),
        out_shape=jax.ShapeDtypeStruct((n, t, HEADS * LANES), F32),
        scratch_shapes=[pltpu.VMEM((rows, LANES), F32), pltpu.VMEM((rows, LANES), F32), pltpu.VMEM((rows, LANES), F32)],
        compiler_params=_params(3),
        name="moba_prompt",
    )(mq, mkv)


def _compress(load_rows, n_chunks, pe_ref, w1_ref, w2_ref):
    lo = jnp.zeros((n_chunks, 2 * NSA_PHI_HIDDEN), F32)
    hi = jnp.zeros((n_chunks, 2 * NSA_PHI_HIDDEN), F32)
    for r in range(NSA_CMP_STRIDE):
        x = load_rows(r)
        lo = lo + _mm(x + pe_ref[r:r + 1, :], w1_ref[r])
        hi = hi + _mm(x + pe_ref[NSA_CMP_STRIDE + r:NSA_CMP_STRIDE + r + 1, :], w1_ref[NSA_CMP_STRIDE + r])
    hidden = lo + pltpu.roll(hi, n_chunks - 1, 0)
    return _mm(hidden * _sigmoid(hidden), w2_ref[...])


def _nsa_compress_kernel(rows_ref, pe_ref, w1_ref, w2_ref, out_ref, *, n_chunks):
    out_ref[...] = _compress(lambda r: rows_ref[pl.ds(r, n_chunks, stride=NSA_CMP_STRIDE), :],
                             n_chunks, pe_ref, w1_ref, w2_ref)


def nsa_compress_prompt(cmp_rows, pe, w1, w2):
    n, t, _ = cmp_rows.shape
    n_chunks = t // NSA_CMP_STRIDE
    return pl.pallas_call(
        functools.partial(_nsa_compress_kernel, n_chunks=n_chunks),
        grid=(n,),
        in_specs=[pl.BlockSpec((None, t, LANES), lambda b: (b, 0, 0)), _const_spec(pe.shape),
                  _const_spec(w1.shape), _const_spec(w2.shape)],
        out_specs=pl.BlockSpec((None, n_chunks, LANES), lambda b: (b, 0, 0)),
        out_shape=jax.ShapeDtypeStruct((n, n_chunks, LANES), F32),
        compiler_params=_params(1),
        name="nsa_compress",
    )(cmp_rows, pe, w1, w2)


def _nsa_select(p_sum, q_pos, ovl_ref, rows_q, width, n_sel_blocks):
    imp = _mm(p_sum, ovl_ref[...])
    blk = _iota((rows_q, width), 1)
    cur = q_pos >> SEL_SHIFT
    forced = (blk == 0) | (blk == cur) | (blk == cur - 1)
    avail = (blk * NSA_SEL_BLOCK <= q_pos) & (blk < n_sel_blocks)
    score = jnp.where(avail, jnp.where(forced, SEL_FORCE, imp), NEG_INF)
    return avail & (_rank_before(score, n_sel_blocks) < NSA_SEL_TOPN)


def _nsa_prompt_kernel(qraw_ref, qrot_ref, gate_ref, ckcv_ref, rows_ref, win_ref, ovl_ref, o_ref,
                       m_sc, l_sc, acc_sc, *, tq, tk, n_cmp, n_sel_blocks):
    i = pl.program_id(1)
    nc_pad = ckcv_ref.shape[0]
    gate = gate_ref[...]
    q_raw = (_stack_heads(qraw_ref, LANES) * NSA_SCALE).astype(BF16)
    q_rot = _stack_heads(qrot_ref, LANES) * NSA_SCALE

    ckcv = ckcv_ref[...].astype(BF16)
    c_idx = _iota((tq, nc_pad), 1)
    vis_c = (c_idx * NSA_CMP_STRIDE + (NSA_CMP_LEN - 1) <= i * tq + _iota((tq, nc_pad), 0)) & (c_idx < n_cmp)
    o_c, p_c = _attend_once(q_raw, ckcv, ckcv, vis_c)
    p_sum = p_c[0]
    for h in range(1, HEADS):
        p_sum = p_sum + p_c[h]

    width = ovl_ref.shape[1]
    blk = _iota((tq, width), 1)
    sel = _nsa_select(p_sum, i * tq + _iota((tq, width), 0), ovl_ref, tq, width, n_sel_blocks)
    flags = pltpu.roll(jnp.where((blk < n_sel_blocks) & ~sel, 1.0, 0.0), DH, 1)
    q_aug = (q_rot + _tile_rows(flags, HEADS)).astype(BF16)
    _online_init(m_sc, l_sc, acc_sc)

    def tile(t, vis):
        kv = rows_ref[pl.ds(pl.multiple_of(t * tk, tk), tk), :]
        key_block = (t * tk + _iota((tk, LANES), 0)) >> SEL_SHIFT
        _flash_update(m_sc, l_sc, acc_sc, q_aug, _block_bias_keys(kv, key_block), kv.astype(BF16), vis)

    diag = (i * tq) // tk
    causal = diag * tk + _iota((tq, tk), 1) <= i * tq + _iota((tq, tk), 0)
    tile(diag, lambda c: causal)
    lax.fori_loop(0, diag, lambda t, carry: tile(t, None) or carry, 0)
    o_s = _online_finish(l_sc, acc_sc)

    slab = NSA_WINDOW + tq
    start = jnp.maximum(i * tq - NSA_WINDOW, 0)
    kv_w = win_ref[pl.ds(pl.multiple_of(start, tq), slab), :].astype(BF16)
    dist = i * tq + _iota((tq, slab), 0) - (start + _iota((tq, slab), 1))
    o_w, _ = _attend_once(q_rot.astype(BF16), kv_w, kv_w, (dist >= 0) & (dist <= NSA_WINDOW))

    for h in range(HEADS):
        r = slice(h * tq, (h + 1) * tq)
        o_ref[:, h * LANES:(h + 1) * LANES] = (gate[:, h:h + 1] * o_c[r] + gate[:, HEADS + h:HEADS + h + 1] * o_s[r]
                                               + gate[:, 2 * HEADS + h:2 * HEADS + h + 1] * o_w[r])


def _overlap_matrix(nc_pad, n_cmp, width):
    start = np.arange(nc_pad)[:, None] * NSA_CMP_STRIDE
    j0 = np.arange(width)[None, :] * NSA_SEL_BLOCK
    ovl = (start < j0 + NSA_SEL_BLOCK) & (start + NSA_CMP_LEN > j0) & (np.arange(nc_pad)[:, None] < n_cmp)
    return jnp.asarray(ovl, BF16)


def nsa_prompt(nq_raw, nq_rot, gate, ckcv, slc_rows, win_rows, *, tq=ROW_CHUNK, tk=512):
    n, t, _ = nq_raw.shape
    tk = min(tk, t)
    nc_pad = t // NSA_CMP_STRIDE
    n_cmp = (t - NSA_CMP_LEN) // NSA_CMP_STRIDE + 1
    n_sel_blocks = t // NSA_SEL_BLOCK
    assert tq == ROW_CHUNK and tk % tq == 0 and n_sel_blocks <= DH
    assert t >= NSA_WINDOW + tq and NSA_WINDOW % tq == 0
    width = LANES
    ovl = _overlap_matrix(nc_pad, n_cmp, width)
    rows = HEADS * tq
    qspec = pl.BlockSpec((None, tq, HEADS * LANES), lambda b, i: (b, i, 0))
    return pl.pallas_call(
        functools.partial(_nsa_prompt_kernel, tq=tq, tk=tk, n_cmp=n_cmp, n_sel_blocks=n_sel_blocks),
        grid=(n, t // tq),
        in_specs=[qspec, qspec, pl.BlockSpec((None, tq, LANES), lambda b, i: (b, i, 0)),
                  pl.BlockSpec((None, nc_pad, LANES), lambda b, i: (b, 0, 0)),
                  pl.BlockSpec((None, t, LANES), lambda b, i: (b, 0, 0)),
                  pl.BlockSpec((None, t, LANES), lambda b, i: (b, 0, 0)),
                  _const_spec(ovl.shape)],
        out_specs=qspec,
        out_shape=jax.ShapeDtypeStruct((n, t, HEADS * LANES), F32),
        scratch_shapes=[pltpu.VMEM((rows, LANES), F32), pltpu.VMEM((rows, LANES), F32), pltpu.VMEM((rows, LANES), F32)],
        compiler_params=_params(2),
        name="nsa_prompt",
    )(nq_raw, nq_rot, gate, ckcv, slc_rows, win_rows, ovl)


def _pad_rows(x, n):
    return jnp.concatenate([x, jnp.zeros((n - x.shape[0], x.shape[1]), x.dtype)], axis=0)


def _all_pages(pages, lo, hi):
    return jnp.concatenate([pg[lo:hi, :] for pg in pages], axis=1).astype(BF16)


def _block_bias_matrix(width, n_blocks, block, n_tokens):
    blk = np.arange(width)[:, None]
    tok = np.arange(n_tokens)[None, :]
    return jnp.asarray(np.where((tok // block == blk) & (blk < n_blocks), NEG_INF, 0.0), BF16)


def _paged_attend(s_past, v_t, s_new, vis_new, v_new):
    ps, ps_new = [], []
    for c in range(s_past.shape[0] // 8):
        r = slice(8 * c, 8 * c + 8)
        sp, vn = s_past[r], vis_new[r]
        sn = jnp.where(vn, s_new[r], NEG_INF)
        m = jnp.maximum(jnp.max(sp, axis=1, keepdims=True), jnp.max(sn, axis=1, keepdims=True))
        e, en = jnp.exp(sp - m), jnp.where(vn, jnp.exp(sn - m), 0.0)
        inv = 1.0 / jnp.maximum(jnp.sum(e, axis=1, keepdims=True) + jnp.sum(en, axis=1, keepdims=True), 1e-30)
        ps.append((e * inv).astype(BF16))
        ps_new.append((en * inv).astype(BF16))
    return _mm_nt(jnp.concatenate(ps, axis=0), v_t) + _mm(jnp.concatenate(ps_new, axis=0), v_new)


def _new_token_vis(rows, s_len):
    return _iota((rows, LANES), 1) <= (_iota((rows, LANES), 0) & (s_len - 1))


def _page_specs(n_pages, rows, layer):
    return [pl.BlockSpec((None, None, rows, LANES), functools.partial(lambda b, pt, p: (layer, pt[b, p], 0, 0), p=p))
            for p in range(n_pages)]


def _moba_sample_kernel(pt_ref, q_ref, kvn_ref, bias_ref, *rest, n_pages, s_len, page):
    pages, (o_ref,) = rest[:n_pages], rest[n_pages:]
    rows = HEADS * s_len
    qb = (_stack_heads(q_ref, LANES) * MOBA_SCALE).astype(BF16)
    n_blocks = n_pages * page // MOBA_BLOCK
    s_raw = _mm(qb, _all_pages(pages, 0, LANES))
    lane = _iota((rows, LANES), 1)
    s_gate = jnp.full((rows, LANES), NEG_INF, F32)
    for blk in range(n_blocks):
        tot = jnp.sum(s_raw[:, blk * MOBA_BLOCK:(blk + 1) * MOBA_BLOCK], axis=1, keepdims=True)
        s_gate = jnp.where(lane == blk, tot * (1.0 / MOBA_BLOCK), s_gate)
    not_sel = jnp.where((lane < n_blocks) & ~(_rank_before(s_gate, n_blocks) < MOBA_TOPK), 1.0, 0.0)

    kv_new = _pad_rows(kvn_ref[...], LANES).astype(BF16)
    o = _paged_attend(s_raw + _mm(not_sel, bias_ref[...]), _all_pages(pages, LANES, 2 * LANES),
                      _mm_nt(qb, kv_new[:, 0:LANES]), _new_token_vis(rows, s_len), kv_new[:, LANES:2 * LANES])
    _unstack_heads(o_ref, o, s_len)


def moba_sample(mq, mkv_new, cache_t, page_table, *, layer):
    b, s_len, _ = mq.shape
    n_pages = page_table.shape[1]
    page = cache_t.shape[3]
    assert (n_pages * page) % MOBA_BLOCK == 0 and MOBA_BLOCK % page == 0 and s_len <= LANES
    n_blocks = n_pages * page // MOBA_BLOCK
    assert MOBA_TOPK <= n_blocks <= LANES
    bias = _block_bias_matrix(LANES, n_blocks, MOBA_BLOCK, n_pages * page)
    row = lambda w: pl.BlockSpec((None, s_len, w), lambda i, pt: (i, 0, 0))
    return pl.pallas_call(
        functools.partial(_moba_sample_kernel, n_pages=n_pages, s_len=s_len, page=page),
        grid_spec=pltpu.PrefetchScalarGridSpec(
            num_scalar_prefetch=1, grid=(b,),
            in_specs=[row(HEADS * LANES), row(256), pl.BlockSpec(bias.shape, lambda i, pt: (0, 0))]
            + _page_specs(n_pages, 256, layer),
            out_specs=row(HEADS * LANES)),
        out_shape=jax.ShapeDtypeStruct((b, s_len, HEADS * LANES), F32),
        compiler_params=_params(1),
        name="moba_sample",
    )(page_table, mq, mkv_new, bias, *([cache_t] * n_pages))


def _mla_sample_kernel(pt_ref, q_ref, kn_ref, *rest, n_pages, s_len, page):
    pages, (o_ref,) = rest[:n_pages], rest[n_pages:]
    rows = HEADS * s_len
    qb = (_stack_heads(q_ref, MLA_QW) * MLA_SCALE).astype(BF16)
    k_new = _pad_rows(kn_ref[...], LANES).astype(BF16)
    lat_t = _all_pages(pages, 0, MLA_KV_LORA)
    s_past = _mm(qb[:, 0:MLA_KV_LORA], lat_t) + _mm(qb[:, MLA_KV_LORA:MLA_ROW], _all_pages(pages, MLA_KV_LORA, MLA_ROW))
    o = _paged_attend(s_past, lat_t, _mm_nt(qb, k_new), _new_token_vis(rows, s_len), k_new[:, 0:MLA_KV_LORA])
    _unstack_heads(o_ref, o, s_len)


def mla_sample(qcat, kcat_new, cache_t, page_table, *, layer):
    b, s_len, _ = qcat.shape
    n_pages = page_table.shape[1]
    page = cache_t.shape[3]
    assert page == LANES and s_len <= LANES
    row = lambda w: pl.BlockSpec((None, s_len, w), lambda i, pt: (i, 0, 0))
    return pl.pallas_call(
        functools.partial(_mla_sample_kernel, n_pages=n_pages, s_len=s_len, page=page),
        grid_spec=pltpu.PrefetchScalarGridSpec(
            num_scalar_prefetch=1, grid=(b,),
            in_specs=[row(HEADS * MLA_QW), row(MLA_QW)] + _page_specs(n_pages, MLA_ROW, layer),
            out_specs=row(HEADS * LANES)),
        out_shape=jax.ShapeDtypeStruct((b, s_len, HEADS * LANES), F32),
        compiler_params=_params(1),
        name="mla_sample",
    )(page_table, qcat, kcat_new, *([cache_t] * n_pages))


def _nsa_sample_kernel(pt_ref, qraw_ref, qrot_ref, gate_ref, slcn_ref, winn_ref, win_ref, pe_ref, w1_ref, w2_ref,
                       ovl_ref, bias_ref, *rest, n_pages, s_len, page, past_len):
    pages, (o_ref, xs_sc) = rest[:n_pages], rest[n_pages:]
    rows = HEADS * s_len
    q_raw = (_stack_heads(qraw_ref, LANES) * NSA_SCALE).astype(BF16)
    q_rot = (_stack_heads(qrot_ref, LANES) * NSA_SCALE).astype(BF16)

    for p in range(n_pages):
        xs_sc[p * page:(p + 1) * page, :] = pages[p][0:LANES, :].T
    n_chunks = past_len // NSA_CMP_STRIDE
    n_cmp = n_chunks - 1
    ckcv = _compress(lambda r: xs_sc[pl.ds(r, n_chunks, stride=NSA_CMP_STRIDE), :],
                     n_chunks, pe_ref, w1_ref, w2_ref).astype(BF16)
    q_pos_c = past_len + (_iota((rows, n_chunks), 0) & (s_len - 1))
    c_idx = _iota((rows, n_chunks), 1)
    vis_c = (c_idx * NSA_CMP_STRIDE + (NSA_CMP_LEN - 1) <= q_pos_c) & (c_idx < n_cmp)
    o_c, (p_c,) = _softmax_parts([(_mm_nt(q_raw, ckcv), vis_c, ckcv, False)])

    width = ovl_ref.shape[1]
    n_sel_blocks = -(-(past_len + s_len) // NSA_SEL_BLOCK)
    q_pos1 = past_len + _iota((s_len, width), 0)
    p_sum = p_c[0:s_len, :]
    for h in range(1, HEADS):
        p_sum = p_sum + p_c[h * s_len:(h + 1) * s_len, :]
    sel = _nsa_select(p_sum, q_pos1, ovl_ref, s_len, width, n_sel_blocks)
    not_sel = jnp.where((_iota((s_len, width), 1) < n_sel_blocks) & ~sel, 1.0, 0.0)
    slc_t = _all_pages(pages, LANES, 2 * LANES)
    s_past = _mm(q_rot, slc_t) + _tile_rows(_mm(not_sel, bias_ref[...]), HEADS)
    new_blk = past_len // NSA_SEL_BLOCK
    sel_new = jnp.where(sel[:, new_blk:new_blk + 1], 1.0, 0.0)
    vis_new = _new_token_vis(rows, s_len) & (_tile_rows(jnp.broadcast_to(sel_new, (s_len, LANES)), HEADS) > 0.5)
    slc_new = _pad_rows(slcn_ref[...], LANES).astype(BF16)
    o_s = _paged_attend(s_past, slc_t, _mm_nt(q_rot, slc_new), vis_new, slc_new)

    wb = win_ref.shape[1]
    q_pos_w = past_len + (_iota((rows, wb), 0) & (s_len - 1))
    w_pos = past_len - wb + _iota((rows, wb), 1)
    dist = q_pos_w - w_pos
    vis_buf = (dist >= 0) & (dist <= NSA_WINDOW) & (w_pos >= 0)
    win_t = win_ref[...].astype(BF16)
    win_new = _pad_rows(winn_ref[...], LANES).astype(BF16)
    o_w, _ = _softmax_parts([(_mm(q_rot, win_t), vis_buf, win_t, True),
                             (_mm_nt(q_rot, win_new), _new_token_vis(rows, s_len), win_new, False)])

    gate = gate_ref[...]
    for h in range(HEADS):
        sl = slice(h * s_len, (h + 1) * s_len)
        o_ref[:, h * LANES:(h + 1) * LANES] = (gate[:, h:h + 1] * o_c[sl]
                                               + gate[:, HEADS + h:HEADS + h + 1] * o_s[sl]
                                               + gate[:, 2 * HEADS + h:2 * HEADS + h + 1] * o_w[sl])


def nsa_sample(nq_raw, nq_rot, gate, slc_new, win_new, cache_t, win_t, page_table, pe, w1, w2, *, layer):
    b, s_len, _ = nq_raw.shape
    n_pages = page_table.shape[1]
    page = cache_t.shape[3]
    past_len = n_pages * page
    wb = win_t.shape[3]
    assert page == LANES and page % NSA_SEL_BLOCK == 0 and past_len % NSA_CMP_STRIDE == 0
    assert s_len < NSA_CMP_STRIDE and s_len <= NSA_SEL_BLOCK and wb <= past_len and NSA_WINDOW <= wb + s_len
    n_chunks = past_len // NSA_CMP_STRIDE
    n_sel_blocks = -(-(past_len + s_len) // NSA_SEL_BLOCK)
    width = -(-n_sel_blocks // LANES) * LANES
    ovl = _overlap_matrix(n_chunks, n_chunks - 1, width)
    bias = _block_bias_matrix(width, past_len // NSA_SEL_BLOCK, NSA_SEL_BLOCK, past_len)
    row = lambda w: pl.BlockSpec((None, s_len, w), lambda i, pt: (i, 0, 0))
    const = lambda a: pl.BlockSpec(a.shape, lambda i, pt: (0,) * a.ndim)
    return pl.pallas_call(
        functools.partial(_nsa_sample_kernel, n_pages=n_pages, s_len=s_len, page=page, past_len=past_len),
        grid_spec=pltpu.PrefetchScalarGridSpec(
            num_scalar_prefetch=1, grid=(b,),
            in_specs=[row(HEADS * LANES), row(HEADS * LANES), row(LANES), row(LANES), row(LANES),
                      pl.BlockSpec((None, None, LANES, wb), lambda i, pt: (layer, i, 0, 0)),
                      const(pe), const(w1), const(w2), const(ovl), const(bias)] + _page_specs(n_pages, 256, layer),
            out_specs=row(HEADS * LANES),
            scratch_shapes=[pltpu.VMEM((past_len, LANES), F32)]),
        out_shape=jax.ShapeDtypeStruct((b, s_len, HEADS * LANES), F32),
        compiler_params=_params(1),
        name="nsa_sample",
    )(page_table, nq_raw, nq_rot, gate, slc_new, win_new, win_t, pe, w1, w2, ovl, bias, *([cache_t] * n_pages))


def _partner(d, rot):
    half = rot // 2
    return d + half if d < half else (d - half if d < rot else -1)


def _w_all_columns(q_lane_offset, kv_group_major):
    src = np.full(W_ALL_COLS, -1, np.int64)
    for h in range(HEADS):
        for d in range(DH):
            pd = _partner(d, ROT16)
            mq = h * LANES + q_lane_offset(h) + d
            src[G_MQ + mq] = S_MQ + h * DH + d
            src[G_NQ + h * LANES + d] = S_NQ + h * DH + d
            if pd >= 0:
                src[G_MQR + mq] = S_MQ + h * DH + pd
                src[G_NQR + h * LANES + d] = S_NQ + h * DH + pd
    for g in range(MOBA_KV_HEADS):
        for d in range(DH):
            k_col = g * LANES + d if kv_group_major else g * DH + d
            v_col = g * LANES + DH + d if kv_group_major else LANES + g * DH + d
            src[G_MKV + k_col] = S_MK + g * DH + d
            src[G_MKV + v_col] = S_MV + g * DH + d
            if _partner(d, ROT16) >= 0:
                src[G_MKVR + k_col] = S_MK + g * DH + _partner(d, ROT16)
    src[G_CQ:G_CQ + MLA_Q_LORA] = S_CQ + np.arange(MLA_Q_LORA)
    src[G_CKV:G_CKV + MLA_KV_LORA] = S_CKV + np.arange(MLA_KV_LORA)
    for r in range(MLA_ROPE):
        src[G_KR + r] = S_KR + r
        src[G_KRR + r] = S_KR + _partner(r, MLA_ROPE)
    src[G_NSA:G_NSA + 4 * DH] = S_KC + np.arange(4 * DH)
    src[G_WIN:G_WIN + 2 * DH] = S_KW + np.arange(2 * DH)
    for d in range(ROT16):
        src[G_NSAR + d] = S_KS + _partner(d, ROT16)
        src[G_WINR + d] = S_KW + _partner(d, ROT16)
    src[G_NG:G_NG + 3 * HEADS] = S_NG + np.arange(3 * HEADS)
    return src


def _take_cols(w, src):
    picked = jnp.take(w, jnp.asarray(np.maximum(src, 0)), axis=-1)
    return jnp.where(jnp.asarray(src >= 0), picked, 0.0)


def _pad_head_rows(w, lane_offset):
    src = np.full(HEADS * LANES, -1, np.int64)
    for h in range(HEADS):
        src[h * LANES + lane_offset(h) + np.arange(DH)] = h * DH + np.arange(DH)
    picked = jnp.take(w, jnp.asarray(np.maximum(src, 0)), axis=1)
    return jnp.where(jnp.asarray(src >= 0)[None, :, None], picked, 0.0)


def _block_diag_heads(w):
    l, h, a, b = w.shape
    eye = jnp.eye(h, dtype=w.dtype)
    return (w[:, :, :, None, :] * eye[None, :, None, :, None]).reshape(l, h * a, h * b)


def _mla_query_weights(w_uq, w_uk):
    src = np.zeros(1024, np.int64)
    per = MLA_NOPE + MLA_ROPE
    for h in range(HEADS):
        src[h * MLA_NOPE + np.arange(MLA_NOPE)] = h * per + np.arange(MLA_NOPE)
        for r in range(MLA_ROPE):
            src[512 + h * MLA_ROPE + r] = h * per + MLA_NOPE + r
            src[768 + h * MLA_ROPE + r] = h * per + MLA_NOPE + _partner(r, MLA_ROPE)
    w_uq_all = _take_cols(w_uq, src)
    depth = w_uk.shape[0]
    wk = w_uk.reshape(depth, MLA_KV_LORA, HEADS, MLA_NOPE).transpose(0, 2, 3, 1)
    wk = jnp.pad(wk, ((0, 0), (0, 0), (0, 0), (0, MLA_QW - MLA_KV_LORA)))
    top = _block_diag_heads(wk)
    bottom = np.zeros((HEADS * MLA_ROPE, HEADS * MLA_QW), np.float32)
    for h in range(HEADS):
        for r in range(MLA_ROPE):
            bottom[h * MLA_ROPE + r, h * MLA_QW + MLA_KV_LORA + r] = 1.0
    w_cat = jnp.concatenate([top, jnp.broadcast_to(jnp.asarray(bottom), (depth,) + bottom.shape)], axis=1)
    return w_uq_all.astype(BF16), w_cat.astype(BF16)


def _compress_weights(pe_k, w1_k, w2_k, pe_v, w1_v, w2_v):
    depth = pe_k.shape[0]
    pe = jnp.concatenate([pe_k, pe_v], axis=-1)
    w1 = jnp.stack([w1_k.reshape(depth, NSA_CMP_LEN, DH, NSA_PHI_HIDDEN),
                    w1_v.reshape(depth, NSA_CMP_LEN, DH, NSA_PHI_HIDDEN)], axis=2)
    w1 = _block_diag_heads(w1.reshape(depth * NSA_CMP_LEN, 2, DH, NSA_PHI_HIDDEN))
    w1 = w1.reshape(depth, NSA_CMP_LEN, 2 * DH, 2 * NSA_PHI_HIDDEN)
    w2 = _block_diag_heads(jnp.stack([w2_k, w2_v], axis=1))
    return pe, w1.astype(BF16), w2.astype(BF16)


def _rope_tables(pos):
    posf = pos.astype(F32)[:, None]
    lane = np.arange(LANES)

    def tables(width, rot):
        d = lane % width
        half = rot // 2
        inv_freq = ROPE_THETA ** (-2.0 * jnp.arange(half, dtype=F32) / rot)
        ang = posf * inv_freq[jnp.asarray(d % half)]
        cos, sin = jnp.cos(ang), jnp.sin(ang)
        c = jnp.where(jnp.asarray(d < rot), cos, 1.0)
        s = jnp.where(jnp.asarray(d < half), -sin, jnp.where(jnp.asarray(d < rot), sin, 0.0))
        return c, s

    ca, sa = tables(DH, ROT16)
    cb, sb = tables(MLA_ROPE, MLA_ROPE)
    return ca, sa, cb, sb


def _forward(cfg, x_prompt, x_sample, cache_moba, cache_mla, cache_nsa, state_nsa_win, page_table,
             norm_mix, w_in, mla_q_norm, mla_w_uq, mla_kv_norm, mla_w_uk, mla_w_uv,
             nsa_pe_k, nsa_w1_k, nsa_w2_k, nsa_pe_v, nsa_w1_v, nsa_w2_v,
             w_oa, w_ob, w_oc, w_gate, w_out, norm_ffn, w_ff1, w_ff2, norm_final):
    n, t, b, s_len = cfg.n_seq, cfg.t_len, cfg.dec_batch, cfg.dec_seq
    depth = w_in.shape[0]
    mp, ms = n * t, b * s_len
    tm_p, tm_s = 128, min(128, ms)
    bf = lambda w: w.astype(BF16)
    vec = lambda g: g.reshape(depth, 1, -1)

    value_half = lambda h: DH
    group_lane = lambda h: DH * (h // MOBA_HPG)
    w_all_p = bf(_take_cols(w_in, _w_all_columns(lambda h: 0, True)))
    w_all_s = bf(_take_cols(w_in, _w_all_columns(group_lane, False)))
    w_uq_all, w_cat = _mla_query_weights(mla_w_uq, mla_w_uk)
    w_uv_bd = bf(_block_diag_heads(mla_w_uv.reshape(depth, MLA_KV_LORA, HEADS, DH).transpose(0, 2, 1, 3)))
    w_oa_p, w_oa_s = bf(_pad_head_rows(w_oa, value_half)), bf(_pad_head_rows(w_oa, group_lane))
    w_oc_pad = bf(_pad_head_rows(w_oc, value_half))
    pe, w1c, w2c = _compress_weights(nsa_pe_k, nsa_w1_k, nsa_w2_k, nsa_pe_v, nsa_w1_v, nsa_w2_v)
    w_ob_b, w_gate_b, w_out_b, w_ff1_b, w_ff2_b = bf(w_ob), bf(w_gate), bf(w_out), bf(w_ff1), bf(w_ff2)
    g_mix, g_ffn, g_q, g_kv = vec(norm_mix), vec(norm_ffn), vec(mla_q_norm), vec(mla_kv_norm)
    g_final = norm_final.reshape(1, -1)

    tabs_p = _rope_tables(jnp.arange(t))
    tabs_s = _rope_tables(cfg.past_len + (jnp.arange(tm_s) % s_len))

    pool = cache_moba.shape[1]
    cm_t = cache_moba.transpose(0, 1, 3, 4, 5, 2).reshape(depth, pool, 4 * DH, cfg.page)
    cn_t = cache_nsa.transpose(0, 1, 3, 4, 2).reshape(depth, pool, 4 * DH, cfg.page)
    cl_t = cache_mla.transpose(0, 1, 3, 2)
    win_t = state_nsa_win.transpose(0, 1, 3, 4, 2).reshape(depth, b, 2 * DH, -1)

    xp = x_prompt.reshape(mp, D_MODEL)
    xs = x_sample.reshape(ms, D_MODEL)
    leaves = [[] for _ in range(8)]
    for l in range(depth):
        last = l == depth - 1
        mq, mkv, qcat, kcat, nq_raw, nq_rot, cmp_r, slc_r, win_r, gate = in_proj(
            xp, g_mix[l], w_all_p[l], g_q[l], w_uq_all[l], w_cat[l], g_kv[l], tabs_p, tm=tm_p, kv_group_major=True)
        r3 = lambda a: a.reshape(n, t, -1)
        o_a = moba_prompt(r3(mq), r3(mkv))
        o_b = mla_prompt(r3(qcat), r3(kcat))
        ckcv = nsa_compress_prompt(r3(cmp_r), pe[l], w1c[l], w2c[l])
        o_c = nsa_prompt(r3(nq_raw), r3(nq_rot), r3(gate), ckcv, r3(slc_r), r3(win_r))
        x1 = merge(xp, g_mix[l], w_gate_b[l], o_a.reshape(mp, -1), o_b.reshape(mp, -1), o_c.reshape(mp, -1),
                   w_uv_bd[l], w_oa_p[l], w_ob_b[l], w_oc_pad[l], w_out_b[l], tm=256)
        xp = ffn(x1, g_ffn[l], w_ff1_b[l], w_ff2_b[l], g_final, tm=256, final_norm=last)
        leaves[0].append(mkv.reshape(n, t, MOBA_KV_HEADS, 2, DH).transpose(0, 1, 3, 2, 4))
        leaves[2].append(kcat[:, :MLA_ROW].reshape(n, t, MLA_ROW))
        leaves[4].append(jnp.concatenate([cmp_r, slc_r], axis=1).reshape(n, t, 4, DH))
        leaves[6].append(win_r.reshape(n, t, 2, DH)[:, t - min(NSA_WINDOW, t):])
        mq, mkv, qcat, kcat, nq_raw, nq_rot, cmp_r, slc_r, win_r, gate = in_proj(
            xs, g_mix[l], w_all_s[l], g_q[l], w_uq_all[l], w_cat[l], g_kv[l], tabs_s, tm=tm_s, kv_group_major=False)
        r3 = lambda a: a.reshape(b, s_len, -1)
        o_a = moba_sample(r3(mq), r3(mkv), cm_t, page_table, layer=l)
        o_b = mla_sample(r3(qcat), r3(kcat), cl_t, page_table, layer=l)
        o_c = nsa_sample(r3(nq_raw), r3(nq_rot), r3(gate), r3(slc_r), r3(win_r), cn_t, win_t, page_table,
                         pe[l], w1c[l], w2c[l], layer=l)
        x1 = merge(xs, g_mix[l], w_gate_b[l], o_a.reshape(ms, -1), o_b.reshape(ms, -1), o_c.reshape(ms, -1),
                   w_uv_bd[l], w_oa_s[l], w_ob_b[l], w_oc_pad[l], w_out_b[l], tm=min(256, ms))
        xs = ffn(x1, g_ffn[l], w_ff1_b[l], w_ff2_b[l], g_final, tm=min(256, ms), final_norm=last)
        leaves[1].append(mkv.reshape(b, s_len, 2, MOBA_KV_HEADS, DH))
        leaves[3].append(kcat[:, :MLA_ROW].reshape(b, s_len, MLA_ROW))
        leaves[5].append(jnp.concatenate([cmp_r, slc_r], axis=1).reshape(b, s_len, 4, DH))
        win_full = jnp.concatenate([state_nsa_win[l], win_r.reshape(b, s_len, 2, DH)], axis=1)
        leaves[7].append(win_full[:, win_full.shape[1] - min(NSA_WINDOW, win_full.shape[1]):])
    return (xp.reshape(n, t, D_MODEL), xs.reshape(b, s_len, D_MODEL)) + tuple(jnp.stack(v) for v in leaves)


def kernel(x_prompt, x_sample, cache_moba, cache_mla, cache_nsa, state_nsa_win, page_table, norm_mix, w_in,
           mla_q_norm, mla_w_uq, mla_kv_norm, mla_w_uk, mla_w_uv, nsa_pe_k, nsa_w1_k, nsa_w2_k, nsa_pe_v,
           nsa_w1_v, nsa_w2_v, w_oa, w_ob, w_oc, w_gate, w_out, norm_ffn, w_ff1, w_ff2, norm_final):
    return _forward(PROD, x_prompt, x_sample, cache_moba, cache_mla, cache_nsa, state_nsa_win, page_table,
                    norm_mix, w_in, mla_q_norm, mla_w_uq, mla_kv_norm, mla_w_uk, mla_w_uv,
                    nsa_pe_k, nsa_w1_k, nsa_w2_k, nsa_pe_v, nsa_w1_v, nsa_w2_v,
                    w_oa, w_ob, w_oc, w_gate, w_out, norm_ffn, w_ff1, w_ff2, norm_final)
```

```python
import collections
import functools

import numpy as np
import jax
import jax.numpy as jnp
from jax import lax
from jax.experimental import pallas as pl
from jax.experimental.pallas import tpu as pltpu

F32 = jnp.float32
BF16 = jnp.bfloat16

D_MODEL = 1024
BATCH = 4
SEQ = 4096
DEPTH = 4
DEC_BATCH = 128
DEC_SEQ = 8
PAST_LEN = 8192
PAGE_SIZE = 128

ROPE_THETA = 500000.0
NORM_EPS = 1e-6
NEG_INF = -1e30
SEL_FORCE = 1e30

HEADS = 8
DH = 64
LANES = 128
ROW_CHUNK = 128
MOBA_KV_HEADS = 2
MOBA_HPG = HEADS // MOBA_KV_HEADS
MOBA_BLOCK = 256
MOBA_TOPK = 3
MOBA_SCALE = DH ** -0.5
ROT16 = 16

MLA_NOPE = 64
MLA_ROPE = 32
MLA_Q_LORA = 256
MLA_KV_LORA = 128
MLA_ROW = MLA_KV_LORA + MLA_ROPE
MLA_QW = 256
MLA_SCALE = (MLA_NOPE + MLA_ROPE) ** -0.5

NSA_CMP_LEN = 32
NSA_CMP_STRIDE = 16
NSA_SEL_BLOCK = 64
SEL_SHIFT = 6
NSA_SEL_TOPN = 16
NSA_WINDOW = 512
NSA_PHI_HIDDEN = 128
NSA_SCALE = DH ** -0.5

D_FF = 4 * D_MODEL

IN_WIDTHS = (HEADS * DH, MOBA_KV_HEADS * DH, MOBA_KV_HEADS * DH, MLA_Q_LORA, MLA_KV_LORA, MLA_ROPE,
             HEADS * DH, DH, DH, DH, DH, DH, DH, 3 * HEADS)
(S_MQ, S_MK, S_MV, S_CQ, S_CKV, S_KR, S_NQ, S_KC, S_VC, S_KS, S_VS, S_KW, S_VW, S_NG) = (
    int(v) for v in np.concatenate([[0], np.cumsum(IN_WIDTHS)[:-1]]))

G_MQ = 0
G_MKV = 1024
G_CQ = 1280
G_CKV = 1536
G_KR = 1664
G_NQ = 1792
G_NSA = 2816
G_WIN = 3072
G_NG = 3200
W_ALL_COLS = 3328

VMEM_LIMIT = 56 * 1024 * 1024

Cfg = collections.namedtuple("Cfg", "n_seq t_len dec_batch dec_seq past_len page")
PROD = Cfg(BATCH, SEQ, DEC_BATCH, DEC_SEQ, PAST_LEN, PAGE_SIZE)


def _params(n_axes):
    return pltpu.CompilerParams(dimension_semantics=("arbitrary",) * n_axes, vmem_limit_bytes=VMEM_LIMIT)


def _rms(x, g):
    return x * lax.rsqrt(jnp.mean(x * x, axis=-1, keepdims=True) + NORM_EPS) * g


def _mm(a, b):
    return jnp.dot(a.astype(BF16), b.astype(BF16), preferred_element_type=F32)


def _mm_nt(a, b):
    return lax.dot_general(a.astype(BF16), b.astype(BF16), (((1,), (1,)), ((), ())),
                           preferred_element_type=F32)


def _sigmoid(x):
    return 1.0 / (1.0 + jnp.exp(-x))


def _tile_lanes(t, n):
    return t if n == 1 else jnp.concatenate([t] * n, axis=1)


def _tile_rows(t, n):
    return t if n == 1 else jnp.concatenate([t] * n, axis=0)


def _iota(shape, dim):
    return lax.broadcasted_iota(jnp.int32, shape, dim)


def _const_spec(shape):
    nd = len(shape)
    return pl.BlockSpec(shape, lambda *_: (0,) * nd)


def _in_proj_kernel(x_ref, g_ref, wall_ref, qn_ref, wuq_ref, wcat_ref, kvn_ref,
                    ca_ref, sa_ref, cb_ref, sb_ref,
                    mq_ref, mkv_ref, qcat_ref, kcat_ref, nqraw_ref, nqrot_ref, cmp_ref, slc_ref, win_ref, gate_ref,
                    *, kv_group_major):
    hb = _rms(x_ref[...], g_ref[...]).astype(BF16)

    def proj(lo, width):
        return jnp.dot(hb, wall_ref[:, lo:lo + width], preferred_element_type=F32)

    ca, sa, cb, sb = ca_ref[...], sa_ref[...], cb_ref[...], sb_ref[...]
    rows = hb.shape[0]
    lane = _iota((rows, LANES), 1)

    def rotary(z, cos, sin, width, half, lanes=None):
        first = (lane & (width - 1)) < half
        out = []
        for b in range(z.shape[1] // LANES):
            zb = z[:, b * LANES:(b + 1) * LANES]
            keep = lanes(b)
            if keep is False:
                out.append(zb)
                continue
            partner = jnp.where(first, pltpu.roll(zb, LANES - half, 1), pltpu.roll(zb, half, 1))
            rb = zb * cos + partner * sin
            out.append(rb if keep is True else jnp.where(keep, rb, zb))
        return out[0] if len(out) == 1 else jnp.concatenate(out, axis=1)

    key_half = lambda b: lane < DH
    rope16 = lambda z, lanes=lambda b: True: rotary(z, ca, sa, DH, ROT16 // 2, lanes)
    mq_ref[...] = rope16(proj(G_MQ, 1024))
    mkv_ref[...] = rope16(proj(G_MKV, 256), key_half if kv_group_major else (lambda b: b == 0))

    cqn = _rms(proj(G_CQ, MLA_Q_LORA), qn_ref[...]).astype(BF16)
    qall = jnp.dot(cqn, wuq_ref[...], preferred_element_type=F32)
    rope32 = lambda z: rotary(z, cb, sb, MLA_ROPE, MLA_ROPE // 2, lambda b: True)
    qin = jnp.concatenate([qall[:, :512], rope32(qall[:, 512:768])], axis=1).astype(BF16)
    qcat_ref[...] = jnp.dot(qin, wcat_ref[...], preferred_element_type=F32)
    kcat_ref[:, 0:LANES] = _rms(proj(G_CKV, MLA_KV_LORA), kvn_ref[...])
    kcat_ref[:, LANES:2 * LANES] = jnp.where(lane < MLA_ROPE, rope32(proj(G_KR, LANES)), 0.0)

    nq = proj(G_NQ, 1024)
    nqraw_ref[...] = nq
    nqrot_ref[...] = rope16(nq)
    cmp_ref[...] = proj(G_NSA, LANES)
    slc_ref[...] = rope16(proj(G_NSA + LANES, LANES), key_half)
    win_ref[...] = rope16(proj(G_WIN, LANES), key_half)
    gate_ref[...] = _sigmoid(proj(G_NG, LANES))


def in_proj(x, g_mix, w_all, q_norm, w_uq_all, w_cat, kv_norm, tabs, *, tm, kv_group_major):
    m = x.shape[0]
    ca, sa, cb, sb = tabs
    n_per = ca.shape[0] // tm
    row = lambda w: pl.BlockSpec((tm, w), lambda i: (i, 0))
    tab = pl.BlockSpec((tm, LANES), lambda i: (i % n_per, 0))
    widths = (1024, 256, HEADS * MLA_QW, 256, 1024, 1024, LANES, LANES, LANES, LANES)
    return pl.pallas_call(
        functools.partial(_in_proj_kernel, kv_group_major=kv_group_major),
        grid=(m // tm,),
        in_specs=[row(D_MODEL), _const_spec((1, D_MODEL)), _const_spec((D_MODEL, W_ALL_COLS)),
                  _const_spec((1, MLA_Q_LORA)), _const_spec((MLA_Q_LORA, 768)),
                  _const_spec((768, HEADS * MLA_QW)), _const_spec((1, MLA_KV_LORA)), tab, tab, tab, tab],
        out_specs=[row(w) for w in widths],
        out_shape=[jax.ShapeDtypeStruct((m, w), F32) for w in widths],
        compiler_params=_params(1),
        name="in_proj",
    )(x, g_mix, w_all, q_norm, w_uq_all, w_cat, kv_norm, ca, sa, cb, sb)


def _merge_kernel(x_ref, g_ref, wgate_ref, oa_ref, ob_ref, oc_ref, wuv_ref, woa_ref, wob_ref, woc_ref,
                  wout_ref, out_ref):
    x = x_ref[...]
    hb = _rms(x, g_ref[...]).astype(BF16)

    def gate(k):
        return _sigmoid(jnp.dot(hb, wgate_ref[:, k * D_MODEL:(k + 1) * D_MODEL], preferred_element_type=F32))

    o_b = _mm(ob_ref[...], wuv_ref[...])
    merged = gate(0) * _mm(oa_ref[...], woa_ref[...])
    merged = merged + gate(1) * _mm(o_b, wob_ref[...])
    merged = merged + gate(2) * _mm(oc_ref[...], woc_ref[...])
    out_ref[...] = x + _mm(merged, wout_ref[...])


def merge(x, g_mix, w_gate, o_a, o_b_lat, o_c, w_uv_bd, w_oa, w_ob, w_oc, w_out, *, tm):
    m = x.shape[0]
    row = lambda w: pl.BlockSpec((tm, w), lambda i: (i, 0))
    return pl.pallas_call(
        _merge_kernel,
        grid=(m // tm,),
        in_specs=[row(D_MODEL), _const_spec((1, D_MODEL)), _const_spec((D_MODEL, 3 * D_MODEL)),
                  row(1024), row(1024), row(1024), _const_spec((1024, 512)),
                  _const_spec((1024, D_MODEL)), _const_spec((512, D_MODEL)), _const_spec((1024, D_MODEL)),
                  _const_spec((D_MODEL, D_MODEL))],
        out_specs=row(D_MODEL),
        out_shape=jax.ShapeDtypeStruct((m, D_MODEL), F32),
        compiler_params=_params(1),
        name="merge",
    )(x, g_mix, w_gate, o_a, o_b_lat, o_c, w_uv_bd, w_oa, w_ob, w_oc, w_out)


def _ffn_kernel(x_ref, g_ref, w1_ref, w2_ref, gf_ref, out_ref, *, final_norm):
    x = x_ref[...]
    hb = _rms(x, g_ref[...]).astype(BF16)
    acc = x
    for c in range(D_FF // D_MODEL):
        u = jnp.maximum(jnp.dot(hb, w1_ref[:, c * D_MODEL:(c + 1) * D_MODEL], preferred_element_type=F32), 0.0)
        acc = acc + jnp.dot((u * u).astype(BF16), w2_ref[c * D_MODEL:(c + 1) * D_MODEL, :],
                            preferred_element_type=F32)
    out_ref[...] = _rms(acc, gf_ref[...]) if final_norm else acc


def ffn(x, g_ffn, w1, w2, g_final, *, tm, final_norm):
    m = x.shape[0]
    row = pl.BlockSpec((tm, D_MODEL), lambda i: (i, 0))
    return pl.pallas_call(
        functools.partial(_ffn_kernel, final_norm=final_norm),
        grid=(m // tm,),
        in_specs=[row, _const_spec((1, D_MODEL)), _const_spec((D_MODEL, D_FF)), _const_spec((D_FF, D_MODEL)),
                  _const_spec((1, D_MODEL))],
        out_specs=row,
        out_shape=jax.ShapeDtypeStruct((m, D_MODEL), F32),
        compiler_params=_params(1),
        name="ffn",
    )(x, g_ffn, w1, w2, g_final)


def _stack_heads(ref, width):
    return jnp.concatenate([ref[:, h * width:(h + 1) * width] for h in range(HEADS)], axis=0)


def _unstack_heads(out_ref, o, rows):
    for h in range(HEADS):
        out_ref[:, h * LANES:(h + 1) * LANES] = o[h * rows:(h + 1) * rows, :]


def _softmax_parts(parts):
    m = None
    for s, mask, _, _ in parts:
        mt = jnp.max(jnp.where(mask, s, NEG_INF), axis=1, keepdims=True)
        m = mt if m is None else jnp.maximum(m, mt)
    l, es = None, []
    for s, mask, _, _ in parts:
        e = jnp.where(mask, jnp.exp(s - m), 0.0)
        es.append(e)
        lt = jnp.sum(e, axis=1, keepdims=True)
        l = lt if l is None else l + lt
    inv = 1.0 / jnp.maximum(l, 1e-30)
    acc, ps = None, []
    for e, (_, _, v, v_t) in zip(es, parts):
        p = e * inv
        ps.append(p)
        o = _mm_nt(p, v) if v_t else _mm(p, v)
        acc = o if acc is None else acc + o
    return acc, ps


def _attend_once(q, k_op, v_op, vis):
    s_all = _mm_nt(q, k_op)
    ps = []
    for c in range(q.shape[0] // ROW_CHUNK):
        s = s_all[c * ROW_CHUNK:(c + 1) * ROW_CHUNK]
        m = jnp.max(jnp.where(vis, s, NEG_INF), axis=1, keepdims=True)
        e = jnp.where(vis, jnp.exp(s - m), 0.0)
        ps.append(e * (1.0 / jnp.maximum(jnp.sum(e, axis=1, keepdims=True), 1e-30)))
    return _mm(jnp.concatenate([p.astype(BF16) for p in ps], axis=0), v_op), ps


def _online_init(m_sc, l_sc, acc_sc):
    m_sc[...] = jnp.full(m_sc.shape, NEG_INF, F32)
    l_sc[...] = jnp.zeros(l_sc.shape, F32)
    acc_sc[...] = jnp.zeros(acc_sc.shape, F32)


def _flash_update(m_sc, l_sc, acc_sc, q, k_op, v_op, vis=None):
    m_all, l_all = m_sc[...], l_sc[...]
    s_all = _mm_nt(q, k_op)
    reps = k_op.shape[0] // LANES
    m_out, l_out, alphas, es = [], [], [], []
    for c in range(q.shape[0] // ROW_CHUNK):
        r = slice(c * ROW_CHUNK, (c + 1) * ROW_CHUNK)
        s = s_all[r]
        if vis is not None:
            s = jnp.where(vis(c), s, NEG_INF)
        m_old = m_all[r]
        m_new = jnp.maximum(m_old, jnp.max(s, axis=1, keepdims=True))
        e = jnp.exp(s - _tile_lanes(m_new, reps))
        alpha = jnp.exp(m_old - m_new)
        m_out.append(m_new)
        l_out.append(alpha * l_all[r] + jnp.sum(e, axis=1, keepdims=True))
        alphas.append(alpha)
        es.append(e.astype(BF16))
    m_sc[...] = jnp.concatenate(m_out, axis=0)
    l_sc[...] = jnp.concatenate(l_out, axis=0)
    acc_sc[...] = jnp.concatenate(alphas, axis=0) * acc_sc[...] + _mm(jnp.concatenate(es, axis=0), v_op)


def _block_bias_keys(kv, key_block):
    lane = _iota(kv.shape, 1)
    return jnp.where(lane < DH, kv, jnp.where(lane - DH == key_block, NEG_INF, 0.0)).astype(BF16)


def _online_finish(l_sc, acc_sc):
    return acc_sc[...] / jnp.maximum(l_sc[...], 1e-30)


def _rank_before(score, n_cols):
    lane = _iota(score.shape, 1)
    cnt = jnp.zeros(score.shape, F32)
    for j in range(n_cols):
        col = score[:, j:j + 1]
        ahead = (col > score) | ((col == score) & (lane > j))
        cnt = cnt + jnp.where(ahead, 1.0, 0.0)
    return cnt


def _mla_prompt_kernel(q_ref, k_ref, o_ref, m_sc, l_sc, acc_sc, *, tq, tk):
    i = pl.program_id(1)
    q = (_stack_heads(q_ref, MLA_QW) * MLA_SCALE).astype(BF16)
    _online_init(m_sc, l_sc, acc_sc)

    def tile(j, vis):
        kt = k_ref[pl.ds(pl.multiple_of(j * tk, tk), tk), :].astype(BF16)
        _flash_update(m_sc, l_sc, acc_sc, q, kt, kt[:, :MLA_KV_LORA], vis)

    diag = (i * tq) // tk
    causal = diag * tk + _iota((tq, tk), 1) <= i * tq + _iota((tq, tk), 0)
    tile(diag, lambda c: causal)
    lax.fori_loop(0, diag, lambda j, carry: tile(j, None) or carry, 0)
    _unstack_heads(o_ref, _online_finish(l_sc, acc_sc), tq)


def mla_prompt(qcat, kcat, *, tq=ROW_CHUNK, tk=512):
    n, t, _ = qcat.shape
    tk = min(tk, t)
    assert tq == ROW_CHUNK and tk % tq == 0
    rows = HEADS * tq
    return pl.pallas_call(
        functools.partial(_mla_prompt_kernel, tq=tq, tk=tk),
        grid=(n, t // tq),
        in_specs=[pl.BlockSpec((None, tq, HEADS * MLA_QW), lambda b, i: (b, i, 0)),
                  pl.BlockSpec((None, t, MLA_QW), lambda b, i: (b, 0, 0))],
        out_specs=pl.BlockSpec((None, tq, HEADS * LANES), lambda b, i: (b, i, 0)),
        out_shape=jax.ShapeDtypeStruct((n, t, HEADS * LANES), F32),
        scratch_shapes=[pltpu.VMEM((rows, LANES), F32), pltpu.VMEM((rows, LANES), F32),
                        pltpu.VMEM((rows, MLA_KV_LORA), F32)],
        compiler_params=_params(2),
        name="mla_prompt",
    )(qcat, kcat)


def _moba_prompt_kernel(q_ref, kv_ref, o_ref, m_sc, l_sc, acc_sc, *, n_blocks):
    i = pl.program_id(2)
    tq = MOBA_BLOCK
    rows = MOBA_HPG * tq
    q = jnp.concatenate([q_ref[:, h * LANES:(h + 1) * LANES] for h in range(MOBA_HPG)], axis=0)
    nb_pad = -(-n_blocks // 8) * 8
    k_mean = jnp.concatenate(
        [jnp.mean(kv_ref[j * tq:(j + 1) * tq, :], axis=0, keepdims=True) for j in range(n_blocks)]
        + ([jnp.zeros((nb_pad - n_blocks, LANES), F32)] if nb_pad > n_blocks else []), axis=0)
    blk = _iota((nb_pad, rows), 0)
    past = blk < i
    s_gate = jnp.where(past, _mm_nt(k_mean, q), NEG_INF)
    cnt = jnp.zeros((nb_pad, rows), F32)
    for j in range(n_blocks):
        other = s_gate[j:j + 1, :]
        cnt = cnt + jnp.where((other > s_gate) | ((other == s_gate) & (blk > j)), 1.0, 0.0)
    not_sel = jnp.where((blk < n_blocks) & ~(past & (cnt < MOBA_TOPK)), 1.0, 0.0)
    flags = jnp.concatenate([jnp.zeros((DH, rows), F32), not_sel, jnp.zeros((LANES - DH - nb_pad, rows), F32)],
                            axis=0).T
    qs = q * MOBA_SCALE
    qb = qs.astype(BF16)
    q_aug = (qs + flags).astype(BF16)

    _online_init(m_sc, l_sc, acc_sc)
    chunks_per_head = tq // ROW_CHUNK
    q_in = _iota((ROW_CHUNK, tq), 0)
    col = _iota((ROW_CHUNK, tq), 1)
    kv_own = kv_ref[pl.ds(pl.multiple_of(i * tq, tq), tq), :].astype(BF16)
    _flash_update(m_sc, l_sc, acc_sc, qb, kv_own, kv_own,
                  lambda c: col <= q_in + (c % chunks_per_head) * ROW_CHUNK)

    def past_blocks(first, count):
        kv_j = kv_ref[pl.ds(pl.multiple_of(first * tq, tq), count * tq), :]
        key_block = first + (_iota((count * tq, LANES), 0) >> (tq.bit_length() - 1))
        _flash_update(m_sc, l_sc, acc_sc, q_aug, _block_bias_keys(kv_j, key_block), kv_j.astype(BF16))

    lax.fori_loop(0, i // 2, lambda jj, carry: past_blocks(2 * jj, 2) or carry, 0)

    @pl.when(i % 2 == 1)
    def _():
        past_blocks(i - 1, 1)
    o = _online_finish(l_sc, acc_sc)
    for h in range(MOBA_HPG):
        o_ref[:, h * LANES:(h + 1) * LANES] = o[h * tq:(h + 1) * tq, :]


def moba_prompt(mq, mkv):
    n, t, _ = mq.shape
    n_blocks = t // MOBA_BLOCK
    rows = MOBA_HPG * MOBA_BLOCK
    gw = MOBA_HPG * LANES
    return pl.pallas_call(
        functools.partial(_moba_prompt_kernel, n_blocks=n_blocks),
        grid=(n, MOBA_KV_HEADS, n_blocks),
        in_specs=[pl.BlockSpec((None, MOBA_BLOCK, gw), lambda b, g, i: (b, i, g)),
                  pl.BlockSpec((None, t, LANES), lambda b, g, i: (b, 0, g))],
        out_specs=pl.BlockSpec((None, MOBA_BLOCK, gw), lambda b, g, i: (b, i, g)),
        out_shape=jax.ShapeDtypeStruct((n, t, HEADS * LANES), F32),
        scratch_shapes=[pltpu.VMEM((rows, LANES), F32), pltpu.VMEM((rows, LANES), F32), pltpu.VMEM((rows, LANES), F32)],
        compiler_params=_params(3),
        name="moba_prompt",
    )(mq, mkv)


def _compress(load_rows, n_chunks, pe_ref, w1_ref, w2_ref):
    lo = jnp.zeros((n_chunks, 2 * NSA_PHI_HIDDEN), F32)
    hi = jnp.zeros((n_chunks, 2 * NSA_PHI_HIDDEN), F32)
    half = NSA_CMP_STRIDE // 2
    for j in range(half):
        x = jnp.concatenate([load_rows(2 * j), load_rows(2 * j + 1)], axis=1)
        lo = lo + _mm(x + pe_ref[j:j + 1, :], w1_ref[j])
        hi = hi + _mm(x + pe_ref[half + j:half + j + 1, :], w1_ref[half + j])
    hidden = lo + pltpu.roll(hi, n_chunks - 1, 0)
    return _mm(hidden * _sigmoid(hidden), w2_ref[...])


def _nsa_compress_kernel(rows_ref, pe_ref, w1_ref, w2_ref, out_ref, *, n_chunks):
    out_ref[...] = _compress(lambda r: rows_ref[pl.ds(r, n_chunks, stride=NSA_CMP_STRIDE), :],
                             n_chunks, pe_ref, w1_ref, w2_ref)


def nsa_compress_prompt(cmp_rows, pe, w1, w2):
    n, t, _ = cmp_rows.shape
    n_chunks = t // NSA_CMP_STRIDE
    return pl.pallas_call(
        functools.partial(_nsa_compress_kernel, n_chunks=n_chunks),
        grid=(n,),
        in_specs=[pl.BlockSpec((None, t, LANES), lambda b: (b, 0, 0)), _const_spec(pe.shape),
                  _const_spec(w1.shape), _const_spec(w2.shape)],
        out_specs=pl.BlockSpec((None, n_chunks, LANES), lambda b: (b, 0, 0)),
        out_shape=jax.ShapeDtypeStruct((n, n_chunks, LANES), F32),
        compiler_params=_params(1),
        name="nsa_compress",
    )(cmp_rows, pe, w1, w2)


def _nsa_select(p_sum, q_pos, ovl_ref, rows_q, width, n_sel_blocks):
    imp = _mm(p_sum, ovl_ref[...])
    blk = _iota((rows_q, width), 1)
    cur = q_pos >> SEL_SHIFT
    forced = (blk == 0) | (blk == cur) | (blk == cur - 1)
    avail = (blk * NSA_SEL_BLOCK <= q_pos) & (blk < n_sel_blocks)
    score = jnp.where(avail, jnp.where(forced, SEL_FORCE, imp), NEG_INF)
    return avail & (_rank_before(score, n_sel_blocks) < NSA_SEL_TOPN)


def _nsa_prompt_kernel(qraw_ref, qrot_ref, gate_ref, ckcv_ref, rows_ref, win_ref, ovl_ref, o_ref,
                       m_sc, l_sc, acc_sc, *, tq, tk, n_cmp, n_sel_blocks):
    i = pl.program_id(1)
    nc_pad = ckcv_ref.shape[0]
    gate = gate_ref[...]
    q_raw = (_stack_heads(qraw_ref, LANES) * NSA_SCALE).astype(BF16)
    q_rot = _stack_heads(qrot_ref, LANES) * NSA_SCALE

    ckcv = ckcv_ref[...].astype(BF16)
    c_idx = _iota((tq, nc_pad), 1)
    vis_c = (c_idx * NSA_CMP_STRIDE + (NSA_CMP_LEN - 1) <= i * tq + _iota((tq, nc_pad), 0)) & (c_idx < n_cmp)
    o_c, p_c = _attend_once(q_raw, ckcv, ckcv, vis_c)
    p_sum = p_c[0]
    for h in range(1, HEADS):
        p_sum = p_sum + p_c[h]

    width = ovl_ref.shape[1]
    blk = _iota((tq, width), 1)
    sel = _nsa_select(p_sum, i * tq + _iota((tq, width), 0), ovl_ref, tq, width, n_sel_blocks)
    flags = pltpu.roll(jnp.where((blk < n_sel_blocks) & ~sel, 1.0, 0.0), DH, 1)
    q_aug = (q_rot + _tile_rows(flags, HEADS)).astype(BF16)
    _online_init(m_sc, l_sc, acc_sc)

    def tile(t, vis):
        kv = rows_ref[pl.ds(pl.multiple_of(t * tk, tk), tk), :]
        key_block = (t * tk + _iota((tk, LANES), 0)) >> SEL_SHIFT
        _flash_update(m_sc, l_sc, acc_sc, q_aug, _block_bias_keys(kv, key_block), kv.astype(BF16), vis)

    diag = (i * tq) // tk
    causal = diag * tk + _iota((tq, tk), 1) <= i * tq + _iota((tq, tk), 0)
    tile(diag, lambda c: causal)
    lax.fori_loop(0, diag, lambda t, carry: tile(t, None) or carry, 0)
    o_s = _online_finish(l_sc, acc_sc)

    slab = NSA_WINDOW + tq
    start = jnp.maximum(i * tq - NSA_WINDOW, 0)
    kv_w = win_ref[pl.ds(pl.multiple_of(start, tq), slab), :].astype(BF16)
    dist = i * tq + _iota((tq, slab), 0) - (start + _iota((tq, slab), 1))
    o_w, _ = _attend_once(q_rot.astype(BF16), kv_w, kv_w, (dist >= 0) & (dist <= NSA_WINDOW))

    for h in range(HEADS):
        r = slice(h * tq, (h + 1) * tq)
        o_ref[:, h * LANES:(h + 1) * LANES] = (gate[:, h:h + 1] * o_c[r] + gate[:, HEADS + h:HEADS + h + 1] * o_s[r]
                                               + gate[:, 2 * HEADS + h:2 * HEADS + h + 1] * o_w[r])


def _overlap_matrix(nc_pad, n_cmp, width):
    start = np.arange(nc_pad)[:, None] * NSA_CMP_STRIDE
    j0 = np.arange(width)[None, :] * NSA_SEL_BLOCK
    ovl = (start < j0 + NSA_SEL_BLOCK) & (start + NSA_CMP_LEN > j0) & (np.arange(nc_pad)[:, None] < n_cmp)
    return jnp.asarray(ovl, BF16)


def nsa_prompt(nq_raw, nq_rot, gate, ckcv, slc_rows, win_rows, *, tq=ROW_CHUNK, tk=512):
    n, t, _ = nq_raw.shape
    tk = min(tk, t)
    nc_pad = t // NSA_CMP_STRIDE
    n_cmp = (t - NSA_CMP_LEN) // NSA_CMP_STRIDE + 1
    n_sel_blocks = t // NSA_SEL_BLOCK
    assert tq == ROW_CHUNK and tk % tq == 0 and n_sel_blocks <= DH
    assert t >= NSA_WINDOW + tq and NSA_WINDOW % tq == 0
    width = LANES
    ovl = _overlap_matrix(nc_pad, n_cmp, width)
    rows = HEADS * tq
    qspec = pl.BlockSpec((None, tq, HEADS * LANES), lambda b, i: (b, i, 0))
    return pl.pallas_call(
        functools.partial(_nsa_prompt_kernel, tq=tq, tk=tk, n_cmp=n_cmp, n_sel_blocks=n_sel_blocks),
        grid=(n, t // tq),
        in_specs=[qspec, qspec, pl.BlockSpec((None, tq, LANES), lambda b, i: (b, i, 0)),
                  pl.BlockSpec((None, nc_pad, LANES), lambda b, i: (b, 0, 0)),
                  pl.BlockSpec((None, t, LANES), lambda b, i: (b, 0, 0)),
                  pl.BlockSpec((None, t, LANES), lambda b, i: (b, 0, 0)),
                  _const_spec(ovl.shape)],
        out_specs=qspec,
        out_shape=jax.ShapeDtypeStruct((n, t, HEADS * LANES), F32),
        scratch_shapes=[pltpu.VMEM((rows, LANES), F32), pltpu.VMEM((rows, LANES), F32), pltpu.VMEM((rows, LANES), F32)],
        compiler_params=_params(2),
        name="nsa_prompt",
    )(nq_raw, nq_rot, gate, ckcv, slc_rows, win_rows, ovl)


def _pad_rows(x, n):
    return jnp.concatenate([x, jnp.zeros((n - x.shape[0], x.shape[1]), x.dtype)], axis=0)


def _all_pages(pages, lo, hi):
    return jnp.concatenate([pg[lo:hi, :] for pg in pages], axis=1).astype(BF16)


def _block_bias_matrix(width, n_blocks, block, n_tokens):
    blk = np.arange(width)[:, None]
    tok = np.arange(n_tokens)[None, :]
    return jnp.asarray(np.where((tok // block == blk) & (blk < n_blocks), NEG_INF, 0.0), BF16)


def _paged_attend(s_past, v_t, s_new, vis_new, v_new, bias=None):
    ps, ps_new = [], []
    for c in range(s_past.shape[0] // 8):
        r = slice(8 * c, 8 * c + 8)
        sp, vn = s_past[r] if bias is None else s_past[r] + bias, vis_new[r]
        sn = jnp.where(vn, s_new[r], NEG_INF)
        m = jnp.maximum(jnp.max(sp, axis=1, keepdims=True), jnp.max(sn, axis=1, keepdims=True))
        e, en = jnp.exp(sp - m), jnp.where(vn, jnp.exp(sn - m), 0.0)
        inv = 1.0 / jnp.maximum(jnp.sum(e, axis=1, keepdims=True) + jnp.sum(en, axis=1, keepdims=True), 1e-30)
        ps.append((e * inv).astype(BF16))
        ps_new.append((en * inv).astype(BF16))
    return _mm_nt(jnp.concatenate(ps, axis=0), v_t) + _mm(jnp.concatenate(ps_new, axis=0), v_new)


def _new_token_vis(rows, s_len):
    return _iota((rows, LANES), 1) <= (_iota((rows, LANES), 0) & (s_len - 1))


SEQ_PER_STEP = 1
NSA_SEQ_PER_STEP = 2


def _page_specs(sps, n_pages, rows, layer):
    return [pl.BlockSpec((None, None, rows, LANES),
                         functools.partial(lambda i, pt, u, p: (layer, pt[sps * i + u, p], 0, 0), u=u, p=p))
            for u in range(sps) for p in range(n_pages)]


def _seq_spec(sps, s_len, width):
    return pl.BlockSpec((sps, s_len, width), lambda i, pt: (i, 0, 0))


def _per_sequence(body, sps, n_seq_in, n_const, n_pages):
    def kernel(pt_ref, *refs):
        seq_in, refs = refs[:n_seq_in], refs[n_seq_in:]
        consts, refs = refs[:n_const], refs[n_const:]
        pages, refs = refs[:sps * n_pages], refs[sps * n_pages:]
        for u in range(sps):
            body(pt_ref, *[r.at[u] for r in seq_in], *consts, *pages[u * n_pages:(u + 1) * n_pages],
                 *[r.at[u] for r in refs])
    return kernel


def _moba_sample_kernel(pt_ref, q_ref, kvn_ref, bias_ref, *rest, n_pages, s_len, page):
    pages, (o_ref,) = rest[:n_pages], rest[n_pages:]
    rows = HEADS * s_len
    qb = (_stack_heads(q_ref, LANES) * MOBA_SCALE).astype(BF16)
    n_blocks = n_pages * page // MOBA_BLOCK
    s_raw = _mm(qb, _all_pages(pages, 0, LANES))
    lane = _iota((rows, LANES), 1)
    s_gate = jnp.full((rows, LANES), NEG_INF, F32)
    for blk in range(n_blocks):
        tot = jnp.sum(s_raw[:, blk * MOBA_BLOCK:(blk + 1) * MOBA_BLOCK], axis=1, keepdims=True)
        s_gate = jnp.where(lane == blk, tot * (1.0 / MOBA_BLOCK), s_gate)
    not_sel = jnp.where((lane < n_blocks) & ~(_rank_before(s_gate, n_blocks) < MOBA_TOPK), 1.0, 0.0)

    kv_new = _pad_rows(kvn_ref[...], LANES).astype(BF16)
    o = _paged_attend(s_raw + _mm(not_sel, bias_ref[...]), _all_pages(pages, LANES, 2 * LANES),
                      _mm_nt(qb, kv_new[:, 0:LANES]), _new_token_vis(rows, s_len), kv_new[:, LANES:2 * LANES])
    _unstack_heads(o_ref, o, s_len)


def moba_sample(mq, mkv_new, cache_t, page_table, *, layer):
    b, s_len, _ = mq.shape
    n_pages = page_table.shape[1]
    page = cache_t.shape[3]
    assert (n_pages * page) % MOBA_BLOCK == 0 and MOBA_BLOCK % page == 0 and s_len <= LANES
    n_blocks = n_pages * page // MOBA_BLOCK
    assert MOBA_TOPK <= n_blocks <= LANES
    bias = _block_bias_matrix(LANES, n_blocks, MOBA_BLOCK, n_pages * page)
    sps = SEQ_PER_STEP
    assert b % sps == 0
    body = functools.partial(_moba_sample_kernel, n_pages=n_pages, s_len=s_len, page=page)
    return pl.pallas_call(
        _per_sequence(body, sps, 2, 1, n_pages),
        grid_spec=pltpu.PrefetchScalarGridSpec(
            num_scalar_prefetch=1, grid=(b // sps,),
            in_specs=[_seq_spec(sps, s_len, HEADS * LANES), _seq_spec(sps, s_len, 256),
                      pl.BlockSpec(bias.shape, lambda i, pt: (0, 0))] + _page_specs(sps, n_pages, 256, layer),
            out_specs=_seq_spec(sps, s_len, HEADS * LANES)),
        out_shape=jax.ShapeDtypeStruct((b, s_len, HEADS * LANES), F32),
        compiler_params=_params(1),
        name="moba_sample",
    )(page_table, mq, mkv_new, bias, *([cache_t] * (sps * n_pages)))


def _mla_sample_kernel(pt_ref, q_ref, kn_ref, *rest, n_pages, s_len, page):
    pages, (o_ref,) = rest[:n_pages], rest[n_pages:]
    rows = HEADS * s_len
    qb = (_stack_heads(q_ref, MLA_QW) * MLA_SCALE).astype(BF16)
    k_new = _pad_rows(kn_ref[...], LANES).astype(BF16)
    lat_t = _all_pages(pages, 0, MLA_KV_LORA)
    s_past = _mm(qb[:, 0:MLA_KV_LORA], lat_t) + _mm(qb[:, MLA_KV_LORA:MLA_ROW], _all_pages(pages, MLA_KV_LORA, MLA_ROW))
    o = _paged_attend(s_past, lat_t, _mm_nt(qb, k_new), _new_token_vis(rows, s_len), k_new[:, 0:MLA_KV_LORA])
    _unstack_heads(o_ref, o, s_len)


def mla_sample(qcat, kcat_new, cache_t, page_table, *, layer):
    b, s_len, _ = qcat.shape
    n_pages = page_table.shape[1]
    page = cache_t.shape[3]
    sps = SEQ_PER_STEP
    assert page == LANES and s_len <= LANES and b % sps == 0
    body = functools.partial(_mla_sample_kernel, n_pages=n_pages, s_len=s_len, page=page)
    return pl.pallas_call(
        _per_sequence(body, sps, 2, 0, n_pages),
        grid_spec=pltpu.PrefetchScalarGridSpec(
            num_scalar_prefetch=1, grid=(b // sps,),
            in_specs=[_seq_spec(sps, s_len, HEADS * MLA_QW), _seq_spec(sps, s_len, MLA_QW)]
            + _page_specs(sps, n_pages, MLA_ROW, layer),
            out_specs=_seq_spec(sps, s_len, HEADS * LANES)),
        out_shape=jax.ShapeDtypeStruct((b, s_len, HEADS * LANES), F32),
        compiler_params=_params(1),
        name="mla_sample",
    )(page_table, qcat, kcat_new, *([cache_t] * (sps * n_pages)))


def _nsa_sample_kernel(pt_ref, qraw_ref, qrot_ref, gate_ref, slcn_ref, winn_ref, win_ref, pe_ref, w1_ref, w2_ref,
                       ovl_ref, *rest, n_pages, s_len, page, past_len):
    pages, (o_ref, xs_sc) = rest[:n_pages], rest[n_pages:]
    rows = HEADS * s_len
    q_raw = (_stack_heads(qraw_ref, LANES) * NSA_SCALE).astype(BF16)
    q_rot = (_stack_heads(qrot_ref, LANES) * NSA_SCALE).astype(BF16)

    for p in range(n_pages):
        xs_sc[p * page:(p + 1) * page, :] = pages[p][0:LANES, :].T
    n_chunks = past_len // NSA_CMP_STRIDE
    n_cmp = n_chunks - 1
    ckcv = _compress(lambda r: xs_sc[pl.ds(r, n_chunks, stride=NSA_CMP_STRIDE), :],
                     n_chunks, pe_ref, w1_ref, w2_ref).astype(BF16)
    q_pos_c = past_len + (_iota((rows, n_chunks), 0) & (s_len - 1))
    c_idx = _iota((rows, n_chunks), 1)
    vis_c = (c_idx * NSA_CMP_STRIDE + (NSA_CMP_LEN - 1) <= q_pos_c) & (c_idx < n_cmp)
    o_c, (p_c,) = _softmax_parts([(_mm_nt(q_raw, ckcv), vis_c, ckcv, False)])

    width = ovl_ref.shape[1]
    n_sel_blocks = -(-(past_len + s_len) // NSA_SEL_BLOCK)
    q_pos1 = past_len + _iota((s_len, width), 0)
    p_sum = p_c[0:s_len, :]
    for h in range(1, HEADS):
        p_sum = p_sum + p_c[h * s_len:(h + 1) * s_len, :]
    sel = _nsa_select(p_sum, q_pos1, ovl_ref, s_len, width, n_sel_blocks)
    hide = jnp.where(sel, 0.0, NEG_INF)
    bpp = page // NSA_SEL_BLOCK
    blk_in_page = _iota((s_len, page), 1) >> SEL_SHIFT
    pieces = []
    for p in range(n_pages):
        piece = jnp.broadcast_to(hide[:, p * bpp:p * bpp + 1], (s_len, page))
        for u in range(1, bpp):
            piece = jnp.where(blk_in_page == u, jnp.broadcast_to(hide[:, p * bpp + u:p * bpp + u + 1], (s_len, page)),
                              piece)
        pieces.append(piece)
    slc_t = _all_pages(pages, LANES, 2 * LANES)
    new_blk = past_len // NSA_SEL_BLOCK
    sel_new = jnp.where(sel[:, new_blk:new_blk + 1], 1.0, 0.0)
    vis_new = _new_token_vis(rows, s_len) & (_tile_rows(jnp.broadcast_to(sel_new, (s_len, LANES)), HEADS) > 0.5)
    slc_new = _pad_rows(slcn_ref[...], LANES).astype(BF16)
    o_s = _paged_attend(_mm(q_rot, slc_t), slc_t, _mm_nt(q_rot, slc_new), vis_new, slc_new,
                        bias=jnp.concatenate(pieces, axis=1))

    wb = win_ref.shape[1]
    q_pos_w = past_len + (_iota((rows, wb), 0) & (s_len - 1))
    w_pos = past_len - wb + _iota((rows, wb), 1)
    dist = q_pos_w - w_pos
    vis_buf = (dist >= 0) & (dist <= NSA_WINDOW) & (w_pos >= 0)
    win_t = win_ref[...].astype(BF16)
    win_new = _pad_rows(winn_ref[...], LANES).astype(BF16)
    o_w, _ = _softmax_parts([(_mm(q_rot, win_t), vis_buf, win_t, True),
                             (_mm_nt(q_rot, win_new), _new_token_vis(rows, s_len), win_new, False)])

    gate = gate_ref[...]
    for h in range(HEADS):
        sl = slice(h * s_len, (h + 1) * s_len)
        o_ref[:, h * LANES:(h + 1) * LANES] = (gate[:, h:h + 1] * o_c[sl]
                                               + gate[:, HEADS + h:HEADS + h + 1] * o_s[sl]
                                               + gate[:, 2 * HEADS + h:2 * HEADS + h + 1] * o_w[sl])


def nsa_sample(nq_raw, nq_rot, gate, slc_new, win_new, cache_t, win_t, page_table, pe, w1, w2, *, layer):
    b, s_len, _ = nq_raw.shape
    n_pages = page_table.shape[1]
    page = cache_t.shape[3]
    past_len = n_pages * page
    wb = win_t.shape[3]
    assert page == LANES and page % NSA_SEL_BLOCK == 0 and past_len % NSA_CMP_STRIDE == 0
    assert s_len < NSA_CMP_STRIDE and s_len <= NSA_SEL_BLOCK and wb <= past_len and NSA_WINDOW <= wb + s_len
    n_chunks = past_len // NSA_CMP_STRIDE
    n_sel_blocks = -(-(past_len + s_len) // NSA_SEL_BLOCK)
    width = -(-n_sel_blocks // LANES) * LANES
    sps = NSA_SEQ_PER_STEP if b % NSA_SEQ_PER_STEP == 0 else 1
    assert s_len == 8
    ovl = _overlap_matrix(n_chunks, n_chunks - 1, width)
    const = lambda a: pl.BlockSpec(a.shape, lambda i, pt: (0,) * a.ndim)
    body = functools.partial(_nsa_sample_kernel, n_pages=n_pages, s_len=s_len, page=page, past_len=past_len)
    return pl.pallas_call(
        _per_sequence(body, sps, 6, 4, n_pages),
        grid_spec=pltpu.PrefetchScalarGridSpec(
            num_scalar_prefetch=1, grid=(b // sps,),
            in_specs=[_seq_spec(sps, s_len, HEADS * LANES), _seq_spec(sps, s_len, HEADS * LANES),
                      _seq_spec(sps, s_len, LANES), _seq_spec(sps, s_len, LANES), _seq_spec(sps, s_len, LANES),
                      pl.BlockSpec((None, sps, LANES, wb), lambda i, pt: (layer, i, 0, 0)),
                      const(pe), const(w1), const(w2), const(ovl)] + _page_specs(sps, n_pages, 256, layer),
            out_specs=_seq_spec(sps, s_len, HEADS * LANES),
            scratch_shapes=[pltpu.VMEM((sps, past_len, LANES), F32)]),
        out_shape=jax.ShapeDtypeStruct((b, s_len, HEADS * LANES), F32),
        compiler_params=_params(1),
        name="nsa_sample",
    )(page_table, nq_raw, nq_rot, gate, slc_new, win_new, win_t, pe, w1, w2, ovl, *([cache_t] * (sps * n_pages)))


def _w_all_columns(q_lane_offset, kv_group_major):
    src = np.full(W_ALL_COLS, -1, np.int64)
    for h in range(HEADS):
        src[G_MQ + h * LANES + q_lane_offset(h) + np.arange(DH)] = S_MQ + h * DH + np.arange(DH)
        src[G_NQ + h * LANES + np.arange(DH)] = S_NQ + h * DH + np.arange(DH)
    for g in range(MOBA_KV_HEADS):
        k_col = g * LANES if kv_group_major else g * DH
        v_col = g * LANES + DH if kv_group_major else LANES + g * DH
        src[G_MKV + k_col + np.arange(DH)] = S_MK + g * DH + np.arange(DH)
        src[G_MKV + v_col + np.arange(DH)] = S_MV + g * DH + np.arange(DH)
    src[G_CQ:G_CQ + MLA_Q_LORA] = S_CQ + np.arange(MLA_Q_LORA)
    src[G_CKV:G_CKV + MLA_KV_LORA] = S_CKV + np.arange(MLA_KV_LORA)
    src[G_KR:G_KR + MLA_ROPE] = S_KR + np.arange(MLA_ROPE)
    src[G_NSA:G_NSA + 4 * DH] = S_KC + np.arange(4 * DH)
    src[G_WIN:G_WIN + 2 * DH] = S_KW + np.arange(2 * DH)
    src[G_NG:G_NG + 3 * HEADS] = S_NG + np.arange(3 * HEADS)
    return src


def _take_cols(w, src):
    picked = jnp.take(w, jnp.asarray(np.maximum(src, 0)), axis=-1)
    return jnp.where(jnp.asarray(src >= 0), picked, 0.0)


def _pad_head_rows(w, lane_offset):
    src = np.full(HEADS * LANES, -1, np.int64)
    for h in range(HEADS):
        src[h * LANES + lane_offset(h) + np.arange(DH)] = h * DH + np.arange(DH)
    picked = jnp.take(w, jnp.asarray(np.maximum(src, 0)), axis=1)
    return jnp.where(jnp.asarray(src >= 0)[None, :, None], picked, 0.0)


def _block_diag_heads(w):
    l, h, a, b = w.shape
    eye = jnp.eye(h, dtype=w.dtype)
    return (w[:, :, :, None, :] * eye[None, :, None, :, None]).reshape(l, h * a, h * b)


def _mla_query_weights(w_uq, w_uk):
    src = np.zeros(768, np.int64)
    per = MLA_NOPE + MLA_ROPE
    for h in range(HEADS):
        src[h * MLA_NOPE + np.arange(MLA_NOPE)] = h * per + np.arange(MLA_NOPE)
        src[512 + h * MLA_ROPE + np.arange(MLA_ROPE)] = h * per + MLA_NOPE + np.arange(MLA_ROPE)
    w_uq_all = _take_cols(w_uq, src)
    depth = w_uk.shape[0]
    wk = w_uk.reshape(depth, MLA_KV_LORA, HEADS, MLA_NOPE).transpose(0, 2, 3, 1)
    wk = jnp.pad(wk, ((0, 0), (0, 0), (0, 0), (0, MLA_QW - MLA_KV_LORA)))
    top = _block_diag_heads(wk)
    bottom = np.zeros((HEADS * MLA_ROPE, HEADS * MLA_QW), np.float32)
    for h in range(HEADS):
        for r in range(MLA_ROPE):
            bottom[h * MLA_ROPE + r, h * MLA_QW + MLA_KV_LORA + r] = 1.0
    w_cat = jnp.concatenate([top, jnp.broadcast_to(jnp.asarray(bottom), (depth,) + bottom.shape)], axis=1)
    return w_uq_all.astype(BF16), w_cat.astype(BF16)


def _compress_weights(pe_k, w1_k, w2_k, pe_v, w1_v, w2_v):
    depth = pe_k.shape[0]
    pe = jnp.concatenate([pe_k, pe_v], axis=-1)
    w1 = jnp.stack([w1_k.reshape(depth, NSA_CMP_LEN, DH, NSA_PHI_HIDDEN),
                    w1_v.reshape(depth, NSA_CMP_LEN, DH, NSA_PHI_HIDDEN)], axis=2)
    w1 = _block_diag_heads(w1.reshape(depth * NSA_CMP_LEN, 2, DH, NSA_PHI_HIDDEN))
    w1 = w1.reshape(depth, NSA_CMP_LEN // 2, 4 * DH, 2 * NSA_PHI_HIDDEN)
    pe = pe.reshape(depth, NSA_CMP_LEN // 2, 4 * DH)
    w2 = _block_diag_heads(jnp.stack([w2_k, w2_v], axis=1))
    return pe, w1.astype(BF16), w2.astype(BF16)


def _rope_tables(pos):
    posf = pos.astype(F32)[:, None]
    lane = np.arange(LANES)

    def tables(width, rot):
        d = lane % width
        half = rot // 2
        inv_freq = ROPE_THETA ** (-2.0 * jnp.arange(half, dtype=F32) / rot)
        ang = posf * inv_freq[jnp.asarray(d % half)]
        cos, sin = jnp.cos(ang), jnp.sin(ang)
        c = jnp.where(jnp.asarray(d < rot), cos, 1.0)
        s = jnp.where(jnp.asarray(d < half), -sin, jnp.where(jnp.asarray(d < rot), sin, 0.0))
        return c, s

    ca, sa = tables(DH, ROT16)
    cb, sb = tables(MLA_ROPE, MLA_ROPE)
    return ca, sa, cb, sb


def _forward(cfg, x_prompt, x_sample, cache_moba, cache_mla, cache_nsa, state_nsa_win, page_table,
             norm_mix, w_in, mla_q_norm, mla_w_uq, mla_kv_norm, mla_w_uk, mla_w_uv,
             nsa_pe_k, nsa_w1_k, nsa_w2_k, nsa_pe_v, nsa_w1_v, nsa_w2_v,
             w_oa, w_ob, w_oc, w_gate, w_out, norm_ffn, w_ff1, w_ff2, norm_final):
    n, t, b, s_len = cfg.n_seq, cfg.t_len, cfg.dec_batch, cfg.dec_seq
    depth = w_in.shape[0]
    mp, ms = n * t, b * s_len
    tm_p, tm_s = 256, min(256, ms)
    bf = lambda w: w.astype(BF16)
    vec = lambda g: g.reshape(depth, 1, -1)

    value_half = lambda h: DH
    group_lane = lambda h: DH * (h // MOBA_HPG)
    w_all_p = bf(_take_cols(w_in, _w_all_columns(lambda h: 0, True)))
    w_all_s = bf(_take_cols(w_in, _w_all_columns(group_lane, False)))
    w_uq_all, w_cat = _mla_query_weights(mla_w_uq, mla_w_uk)
    w_uv_bd = bf(_block_diag_heads(mla_w_uv.reshape(depth, MLA_KV_LORA, HEADS, DH).transpose(0, 2, 1, 3)))
    w_oa_p, w_oa_s = bf(_pad_head_rows(w_oa, value_half)), bf(_pad_head_rows(w_oa, group_lane))
    w_oc_pad = bf(_pad_head_rows(w_oc, value_half))
    pe, w1c, w2c = _compress_weights(nsa_pe_k, nsa_w1_k, nsa_w2_k, nsa_pe_v, nsa_w1_v, nsa_w2_v)
    w_ob_b, w_gate_b, w_out_b, w_ff1_b, w_ff2_b = bf(w_ob), bf(w_gate), bf(w_out), bf(w_ff1), bf(w_ff2)
    g_mix, g_ffn, g_q, g_kv = vec(norm_mix), vec(norm_ffn), vec(mla_q_norm), vec(mla_kv_norm)
    g_final = norm_final.reshape(1, -1)

    tabs_p = _rope_tables(jnp.arange(t))
    tabs_s = _rope_tables(cfg.past_len + (jnp.arange(tm_s) % s_len))

    pool = cache_moba.shape[1]
    cm_t = cache_moba.transpose(0, 1, 3, 4, 5, 2).reshape(depth, pool, 4 * DH, cfg.page)
    cn_t = cache_nsa.transpose(0, 1, 3, 4, 2).reshape(depth, pool, 4 * DH, cfg.page)
    cl_t = cache_mla.transpose(0, 1, 3, 2)
    win_t = state_nsa_win.transpose(0, 1, 3, 4, 2).reshape(depth, b, 2 * DH, -1)

    xp = x_prompt.reshape(mp, D_MODEL)
    xs = x_sample.reshape(ms, D_MODEL)
    leaves = [[] for _ in range(8)]
    for l in range(depth):
        last = l == depth - 1
        mq, mkv, qcat, kcat, nq_raw, nq_rot, cmp_r, slc_r, win_r, gate = in_proj(
            xp, g_mix[l], w_all_p[l], g_q[l], w_uq_all[l], w_cat[l], g_kv[l], tabs_p, tm=tm_p, kv_group_major=True)
        r3 = lambda a: a.reshape(n, t, -1)
        o_a = moba_prompt(r3(mq), r3(mkv))
        o_b = mla_prompt(r3(qcat), r3(kcat))
        ckcv = nsa_compress_prompt(r3(cmp_r), pe[l], w1c[l], w2c[l])
        o_c = nsa_prompt(r3(nq_raw), r3(nq_rot), r3(gate), ckcv, r3(slc_r), r3(win_r))
        x1 = merge(xp, g_mix[l], w_gate_b[l], o_a.reshape(mp, -1), o_b.reshape(mp, -1), o_c.reshape(mp, -1),
                   w_uv_bd[l], w_oa_p[l], w_ob_b[l], w_oc_pad[l], w_out_b[l], tm=256)
        xp = ffn(x1, g_ffn[l], w_ff1_b[l], w_ff2_b[l], g_final, tm=256, final_norm=last)
        leaves[0].append(mkv.reshape(n, t, MOBA_KV_HEADS, 2, DH).transpose(0, 1, 3, 2, 4))
        leaves[2].append(kcat[:, :MLA_ROW].reshape(n, t, MLA_ROW))
        leaves[4].append(jnp.concatenate([cmp_r, slc_r], axis=1).reshape(n, t, 4, DH))
        leaves[6].append(win_r.reshape(n, t, 2, DH)[:, t - min(NSA_WINDOW, t):])
        mq, mkv, qcat, kcat, nq_raw, nq_rot, cmp_r, slc_r, win_r, gate = in_proj(
            xs, g_mix[l], w_all_s[l], g_q[l], w_uq_all[l], w_cat[l], g_kv[l], tabs_s, tm=tm_s, kv_group_major=False)
        r3 = lambda a: a.reshape(b, s_len, -1)
        o_a = moba_sample(r3(mq), r3(mkv), cm_t, page_table, layer=l)
        o_b = mla_sample(r3(qcat), r3(kcat), cl_t, page_table, layer=l)
        o_c = nsa_sample(r3(nq_raw), r3(nq_rot), r3(gate), r3(slc_r), r3(win_r), cn_t, win_t, page_table,
                         pe[l], w1c[l], w2c[l], layer=l)
        x1 = merge(xs, g_mix[l], w_gate_b[l], o_a.reshape(ms, -1), o_b.reshape(ms, -1), o_c.reshape(ms, -1),
                   w_uv_bd[l], w_oa_s[l], w_ob_b[l], w_oc_pad[l], w_out_b[l], tm=min(256, ms))
        xs = ffn(x1, g_ffn[l], w_ff1_b[l], w_ff2_b[l], g_final, tm=min(256, ms), final_norm=last)
        leaves[1].append(mkv.reshape(b, s_len, 2, MOBA_KV_HEADS, DH))
        leaves[3].append(kcat[:, :MLA_ROW].reshape(b, s_len, MLA_ROW))
        leaves[5].append(jnp.concatenate([cmp_r, slc_r], axis=1).reshape(b, s_len, 4, DH))
        win_full = jnp.concatenate([state_nsa_win[l], win_r.reshape(b, s_len, 2, DH)], axis=1)
        leaves[7].append(win_full[:, win_full.shape[1] - min(NSA_WINDOW, win_full.shape[1]):])
    return (xp.reshape(n, t, D_MODEL), xs.reshape(b, s_len, D_MODEL)) + tuple(jnp.stack(v) for v in leaves)


def kernel(x_prompt, x_sample, cache_moba, cache_mla, cache_nsa, state_nsa_win, page_table, norm_mix, w_in,
           mla_q_norm, mla_w_uq, mla_kv_norm, mla_w_uk, mla_w_uv, nsa_pe_k, nsa_w1_k, nsa_w2_k, nsa_pe_v,
           nsa_w1_v, nsa_w2_v, w_oa, w_ob, w_oc, w_gate, w_out, norm_ffn, w_ff1, w_ff2, norm_final):
    return _forward(PROD, x_prompt, x_sample, cache_moba, cache_mla, cache_nsa, state_nsa_win, page_table,
                    norm_mix, w_in, mla_q_norm, mla_w_uq, mla_kv_norm, mla_w_uk, mla_w_uv,
                    nsa_pe_k, nsa_w1_k, nsa_w2_k, nsa_pe_v, nsa_w1_v, nsa_w2_v,
                    w_oa, w_ob, w_oc, w_gate, w_out, norm_ffn, w_ff1, w_ff2, norm_final)
```

```python
import collections
import functools

import numpy as np
import jax
import jax.numpy as jnp
from jax import lax
from jax.experimental import pallas as pl
from jax.experimental.pallas import tpu as pltpu

F32 = jnp.float32
BF16 = jnp.bfloat16

D_MODEL = 1024
BATCH = 4
SEQ = 4096
DEPTH = 4
DEC_BATCH = 128
DEC_SEQ = 8
PAST_LEN = 8192
PAGE_SIZE = 128

ROPE_THETA = 500000.0
NORM_EPS = 1e-6
NEG_INF = -1e30
SEL_FORCE = 1e30

HEADS = 8
DH = 64
LANES = 128
ROW_CHUNK = 128
MOBA_KV_HEADS = 2
MOBA_HPG = HEADS // MOBA_KV_HEADS
MOBA_BLOCK = 256
MOBA_TOPK = 3
MOBA_SCALE = DH ** -0.5
ROT16 = 16

MLA_NOPE = 64
MLA_ROPE = 32
MLA_Q_LORA = 256
MLA_KV_LORA = 128
MLA_ROW = MLA_KV_LORA + MLA_ROPE
MLA_QW = 256
MLA_SCALE = (MLA_NOPE + MLA_ROPE) ** -0.5

NSA_CMP_LEN = 32
NSA_CMP_STRIDE = 16
NSA_SEL_BLOCK = 64
SEL_SHIFT = 6
NSA_SEL_TOPN = 16
NSA_WINDOW = 512
NSA_PHI_HIDDEN = 128
NSA_SCALE = DH ** -0.5

D_FF = 4 * D_MODEL

IN_WIDTHS = (HEADS * DH, MOBA_KV_HEADS * DH, MOBA_KV_HEADS * DH, MLA_Q_LORA, MLA_KV_LORA, MLA_ROPE,
             HEADS * DH, DH, DH, DH, DH, DH, DH, 3 * HEADS)
(S_MQ, S_MK, S_MV, S_CQ, S_CKV, S_KR, S_NQ, S_KC, S_VC, S_KS, S_VS, S_KW, S_VW, S_NG) = (
    int(v) for v in np.concatenate([[0], np.cumsum(IN_WIDTHS)[:-1]]))

G_MQ = 0
G_MKV = 1024
G_CQ = 1280
G_CKV = 1536
G_KR = 1664
G_NQ = 1792
G_NSA = 2816
G_WIN = 3072
G_NG = 3200
W_ALL_COLS = 3328

VMEM_LIMIT = 56 * 1024 * 1024

Cfg = collections.namedtuple("Cfg", "n_seq t_len dec_batch dec_seq past_len page")
PROD = Cfg(BATCH, SEQ, DEC_BATCH, DEC_SEQ, PAST_LEN, PAGE_SIZE)


def _params(n_axes):
    return pltpu.CompilerParams(dimension_semantics=("arbitrary",) * n_axes, vmem_limit_bytes=VMEM_LIMIT)


def _rms(x, g):
    return x * lax.rsqrt(jnp.mean(x * x, axis=-1, keepdims=True) + NORM_EPS) * g


def _mm(a, b):
    return jnp.dot(a.astype(BF16), b.astype(BF16), preferred_element_type=F32)


def _mm_nt(a, b):
    return lax.dot_general(a.astype(BF16), b.astype(BF16), (((1,), (1,)), ((), ())),
                           preferred_element_type=F32)


def _sigmoid(x):
    return 1.0 / (1.0 + jnp.exp(-x))


def _tile_lanes(t, n):
    return t if n == 1 else jnp.concatenate([t] * n, axis=1)


def _tile_rows(t, n):
    return t if n == 1 else jnp.concatenate([t] * n, axis=0)


def _iota(shape, dim):
    return lax.broadcasted_iota(jnp.int32, shape, dim)


def _const_spec(shape):
    nd = len(shape)
    return pl.BlockSpec(shape, lambda *_: (0,) * nd)


def _in_proj_kernel(x_ref, g_ref, wall_ref, qn_ref, wuq_ref, wcat_ref, kvn_ref,
                    ca_ref, sa_ref, cb_ref, sb_ref,
                    mq_ref, mkv_ref, qcat_ref, kcat_ref, nqraw_ref, nqrot_ref, cmp_ref, slc_ref, win_ref, gate_ref,
                    *, kv_group_major):
    hb = _rms(x_ref[...], g_ref[...]).astype(BF16)

    def proj(lo, width):
        return jnp.dot(hb, wall_ref[:, lo:lo + width], preferred_element_type=F32)

    ca, sa, cb, sb = ca_ref[...], sa_ref[...], cb_ref[...], sb_ref[...]
    rows = hb.shape[0]
    lane = _iota((rows, LANES), 1)

    def rotary(z, cos, sin, width, half, lanes=None):
        first = (lane & (width - 1)) < half
        out = []
        for b in range(z.shape[1] // LANES):
            zb = z[:, b * LANES:(b + 1) * LANES]
            keep = lanes(b)
            if keep is False:
                out.append(zb)
                continue
            partner = jnp.where(first, pltpu.roll(zb, LANES - half, 1), pltpu.roll(zb, half, 1))
            rb = zb * cos + partner * sin
            out.append(rb if keep is True else jnp.where(keep, rb, zb))
        return out[0] if len(out) == 1 else jnp.concatenate(out, axis=1)

    key_half = lambda b: lane < DH
    rope16 = lambda z, lanes=lambda b: True: rotary(z, ca, sa, DH, ROT16 // 2, lanes)
    mq_ref[...] = rope16(proj(G_MQ, 1024))
    mkv_ref[...] = rope16(proj(G_MKV, 256), key_half if kv_group_major else (lambda b: b == 0))

    cqn = _rms(proj(G_CQ, MLA_Q_LORA), qn_ref[...]).astype(BF16)
    qall = jnp.dot(cqn, wuq_ref[...], preferred_element_type=F32)
    rope32 = lambda z: rotary(z, cb, sb, MLA_ROPE, MLA_ROPE // 2, lambda b: True)
    qin = jnp.concatenate([qall[:, :512], rope32(qall[:, 512:768])], axis=1).astype(BF16)
    qcat_ref[...] = jnp.dot(qin, wcat_ref[...], preferred_element_type=F32)
    kcat_ref[:, 0:LANES] = _rms(proj(G_CKV, MLA_KV_LORA), kvn_ref[...])
    kcat_ref[:, LANES:2 * LANES] = jnp.where(lane < MLA_ROPE, rope32(proj(G_KR, LANES)), 0.0)

    nq = proj(G_NQ, 1024)
    nqraw_ref[...] = nq
    nqrot_ref[...] = rope16(nq)
    cmp_ref[...] = proj(G_NSA, LANES)
    slc_ref[...] = rope16(proj(G_NSA + LANES, LANES), key_half)
    win_ref[...] = rope16(proj(G_WIN, LANES), key_half)
    gate_ref[...] = _sigmoid(proj(G_NG, LANES))


def in_proj(x, g_mix, w_all, q_norm, w_uq_all, w_cat, kv_norm, tabs, *, tm, kv_group_major):
    m = x.shape[0]
    ca, sa, cb, sb = tabs
    n_per = ca.shape[0] // tm
    row = lambda w: pl.BlockSpec((tm, w), lambda i: (i, 0))
    tab = pl.BlockSpec((tm, LANES), lambda i: (i % n_per, 0))
    widths = (1024, 256, HEADS * MLA_QW, 256, 1024, 1024, LANES, LANES, LANES, LANES)
    return pl.pallas_call(
        functools.partial(_in_proj_kernel, kv_group_major=kv_group_major),
        grid=(m // tm,),
        in_specs=[row(D_MODEL), _const_spec((1, D_MODEL)), _const_spec((D_MODEL, W_ALL_COLS)),
                  _const_spec((1, MLA_Q_LORA)), _const_spec((MLA_Q_LORA, 768)),
                  _const_spec((768, HEADS * MLA_QW)), _const_spec((1, MLA_KV_LORA)), tab, tab, tab, tab],
        out_specs=[row(w) for w in widths],
        out_shape=[jax.ShapeDtypeStruct((m, w), F32) for w in widths],
        compiler_params=_params(1),
        name="in_proj",
    )(x, g_mix, w_all, q_norm, w_uq_all, w_cat, kv_norm, ca, sa, cb, sb)


def _merge_kernel(x_ref, g_ref, wgate_ref, oa_ref, ob_ref, oc_ref, wuv_ref, woa_ref, wob_ref, woc_ref,
                  wout_ref, out_ref):
    x = x_ref[...]
    hb = _rms(x, g_ref[...]).astype(BF16)

    def gate(k):
        return _sigmoid(jnp.dot(hb, wgate_ref[:, k * D_MODEL:(k + 1) * D_MODEL], preferred_element_type=F32))

    o_b = _mm(ob_ref[...], wuv_ref[...])
    merged = gate(0) * _mm(oa_ref[...], woa_ref[...])
    merged = merged + gate(1) * _mm(o_b, wob_ref[...])
    merged = merged + gate(2) * _mm(oc_ref[...], woc_ref[...])
    out_ref[...] = x + _mm(merged, wout_ref[...])


def merge(x, g_mix, w_gate, o_a, o_b_lat, o_c, w_uv_bd, w_oa, w_ob, w_oc, w_out, *, tm):
    m = x.shape[0]
    row = lambda w: pl.BlockSpec((tm, w), lambda i: (i, 0))
    return pl.pallas_call(
        _merge_kernel,
        grid=(m // tm,),
        in_specs=[row(D_MODEL), _const_spec((1, D_MODEL)), _const_spec((D_MODEL, 3 * D_MODEL)),
                  row(1024), row(1024), row(1024), _const_spec((1024, 512)),
                  _const_spec((1024, D_MODEL)), _const_spec((512, D_MODEL)), _const_spec((1024, D_MODEL)),
                  _const_spec((D_MODEL, D_MODEL))],
        out_specs=row(D_MODEL),
        out_shape=jax.ShapeDtypeStruct((m, D_MODEL), F32),
        compiler_params=_params(1),
        name="merge",
    )(x, g_mix, w_gate, o_a, o_b_lat, o_c, w_uv_bd, w_oa, w_ob, w_oc, w_out)


def _ffn_kernel(x_ref, g_ref, w1_ref, w2_ref, gf_ref, out_ref, *, final_norm):
    x = x_ref[...]
    hb = _rms(x, g_ref[...]).astype(BF16)
    acc = x
    for c in range(D_FF // D_MODEL):
        u = jnp.maximum(jnp.dot(hb, w1_ref[:, c * D_MODEL:(c + 1) * D_MODEL], preferred_element_type=F32), 0.0)
        acc = acc + jnp.dot((u * u).astype(BF16), w2_ref[c * D_MODEL:(c + 1) * D_MODEL, :],
                            preferred_element_type=F32)
    out_ref[...] = _rms(acc, gf_ref[...]) if final_norm else acc


def ffn(x, g_ffn, w1, w2, g_final, *, tm, final_norm):
    m = x.shape[0]
    row = pl.BlockSpec((tm, D_MODEL), lambda i: (i, 0))
    return pl.pallas_call(
        functools.partial(_ffn_kernel, final_norm=final_norm),
        grid=(m // tm,),
        in_specs=[row, _const_spec((1, D_MODEL)), _const_spec((D_MODEL, D_FF)), _const_spec((D_FF, D_MODEL)),
                  _const_spec((1, D_MODEL))],
        out_specs=row,
        out_shape=jax.ShapeDtypeStruct((m, D_MODEL), F32),
        compiler_params=_params(1),
        name="ffn",
    )(x, g_ffn, w1, w2, g_final)


def _stack_heads(ref, width):
    return jnp.concatenate([ref[:, h * width:(h + 1) * width] for h in range(HEADS)], axis=0)


def _unstack_heads(out_ref, o, rows):
    for h in range(HEADS):
        out_ref[:, h * LANES:(h + 1) * LANES] = o[h * rows:(h + 1) * rows, :]


def _softmax_parts(parts):
    m = None
    for s, mask, _, _ in parts:
        mt = jnp.max(jnp.where(mask, s, NEG_INF), axis=1, keepdims=True)
        m = mt if m is None else jnp.maximum(m, mt)
    l, es = None, []
    for s, mask, _, _ in parts:
        e = jnp.where(mask, jnp.exp(s - m), 0.0)
        es.append(e)
        lt = jnp.sum(e, axis=1, keepdims=True)
        l = lt if l is None else l + lt
    inv = 1.0 / jnp.maximum(l, 1e-30)
    acc, ps = None, []
    for e, (_, _, v, v_t) in zip(es, parts):
        p = e * inv
        ps.append(p)
        o = _mm_nt(p, v) if v_t else _mm(p, v)
        acc = o if acc is None else acc + o
    return acc, ps


def _attend_once(q, k_op, v_op, vis):
    s_all = _mm_nt(q, k_op)
    ps = []
    for c in range(q.shape[0] // ROW_CHUNK):
        s = s_all[c * ROW_CHUNK:(c + 1) * ROW_CHUNK]
        m = jnp.max(jnp.where(vis, s, NEG_INF), axis=1, keepdims=True)
        e = jnp.where(vis, jnp.exp(s - m), 0.0)
        ps.append(e * (1.0 / jnp.maximum(jnp.sum(e, axis=1, keepdims=True), 1e-30)))
    return _mm(jnp.concatenate([p.astype(BF16) for p in ps], axis=0), v_op), ps


def _online_init(m_sc, l_sc, acc_sc):
    m_sc[...] = jnp.full(m_sc.shape, NEG_INF, F32)
    l_sc[...] = jnp.zeros(l_sc.shape, F32)
    acc_sc[...] = jnp.zeros(acc_sc.shape, F32)


def _flash_update(m_sc, l_sc, acc_sc, q, k_op, v_op, vis=None):
    m_all, l_all = m_sc[...], l_sc[...]
    s_all = _mm_nt(q, k_op)
    reps = k_op.shape[0] // LANES
    m_out, l_out, alphas, es = [], [], [], []
    for c in range(q.shape[0] // ROW_CHUNK):
        r = slice(c * ROW_CHUNK, (c + 1) * ROW_CHUNK)
        s = s_all[r]
        if vis is not None:
            s = jnp.where(vis(c), s, NEG_INF)
        m_old = m_all[r]
        m_new = jnp.maximum(m_old, jnp.max(s, axis=1, keepdims=True))
        e = jnp.exp(s - _tile_lanes(m_new, reps))
        alpha = jnp.exp(m_old - m_new)
        m_out.append(m_new)
        l_out.append(alpha * l_all[r] + jnp.sum(e, axis=1, keepdims=True))
        alphas.append(alpha)
        es.append(e.astype(BF16))
    m_sc[...] = jnp.concatenate(m_out, axis=0)
    l_sc[...] = jnp.concatenate(l_out, axis=0)
    acc_sc[...] = jnp.concatenate(alphas, axis=0) * acc_sc[...] + _mm(jnp.concatenate(es, axis=0), v_op)


def _block_bias_keys(kv, key_block):
    lane = _iota(kv.shape, 1)
    return jnp.where(lane < DH, kv, jnp.where(lane - DH == key_block, NEG_INF, 0.0)).astype(BF16)


def _online_finish(l_sc, acc_sc):
    return acc_sc[...] / jnp.maximum(l_sc[...], 1e-30)


def _rank_before(score, n_cols):
    lane = _iota(score.shape, 1)
    cnt = jnp.zeros(score.shape, F32)
    for j in range(n_cols):
        col = score[:, j:j + 1]
        ahead = (col > score) | ((col == score) & (lane > j))
        cnt = cnt + jnp.where(ahead, 1.0, 0.0)
    return cnt


def _mla_prompt_kernel(q_ref, k_ref, o_ref, m_sc, l_sc, acc_sc, *, tq, tk):
    i = pl.program_id(1)
    q = (_stack_heads(q_ref, MLA_QW) * MLA_SCALE).astype(BF16)
    _online_init(m_sc, l_sc, acc_sc)

    def tile(j, vis):
        kt = k_ref[pl.ds(pl.multiple_of(j * tk, tk), tk), :].astype(BF16)
        _flash_update(m_sc, l_sc, acc_sc, q, kt, kt[:, :MLA_KV_LORA], vis)

    diag = (i * tq) // tk
    causal = diag * tk + _iota((tq, tk), 1) <= i * tq + _iota((tq, tk), 0)
    tile(diag, lambda c: causal)
    lax.fori_loop(0, diag, lambda j, carry: tile(j, None) or carry, 0)
    _unstack_heads(o_ref, _online_finish(l_sc, acc_sc), tq)


def mla_prompt(qcat, kcat, *, tq=ROW_CHUNK, tk=512):
    n, t, _ = qcat.shape
    tk = min(tk, t)
    assert tq == ROW_CHUNK and tk % tq == 0
    rows = HEADS * tq
    return pl.pallas_call(
        functools.partial(_mla_prompt_kernel, tq=tq, tk=tk),
        grid=(n, t // tq),
        in_specs=[pl.BlockSpec((None, tq, HEADS * MLA_QW), lambda b, i: (b, i, 0)),
                  pl.BlockSpec((None, t, MLA_QW), lambda b, i: (b, 0, 0))],
        out_specs=pl.BlockSpec((None, tq, HEADS * LANES), lambda b, i: (b, i, 0)),
        out_shape=jax.ShapeDtypeStruct((n, t, HEADS * LANES), F32),
        scratch_shapes=[pltpu.VMEM((rows, LANES), F32), pltpu.VMEM((rows, LANES), F32),
                        pltpu.VMEM((rows, MLA_KV_LORA), F32)],
        compiler_params=_params(2),
        name="mla_prompt",
    )(qcat, kcat)


def _moba_prompt_kernel(q_ref, kv_ref, o_ref, m_sc, l_sc, acc_sc, *, n_blocks):
    i = pl.program_id(2)
    tq = MOBA_BLOCK
    rows = MOBA_HPG * tq
    q = jnp.concatenate([q_ref[:, h * LANES:(h + 1) * LANES] for h in range(MOBA_HPG)], axis=0)
    nb_pad = -(-n_blocks // 8) * 8
    k_mean = jnp.concatenate(
        [jnp.mean(kv_ref[j * tq:(j + 1) * tq, :], axis=0, keepdims=True) for j in range(n_blocks)]
        + ([jnp.zeros((nb_pad - n_blocks, LANES), F32)] if nb_pad > n_blocks else []), axis=0)
    blk = _iota((nb_pad, rows), 0)
    past = blk < i
    s_gate = jnp.where(past, _mm_nt(k_mean, q), NEG_INF)
    cnt = jnp.zeros((nb_pad, rows), F32)
    for j in range(n_blocks):
        other = s_gate[j:j + 1, :]
        cnt = cnt + jnp.where((other > s_gate) | ((other == s_gate) & (blk > j)), 1.0, 0.0)
    not_sel = jnp.where((blk < n_blocks) & ~(past & (cnt < MOBA_TOPK)), 1.0, 0.0)
    flags = jnp.concatenate([jnp.zeros((DH, rows), F32), not_sel, jnp.zeros((LANES - DH - nb_pad, rows), F32)],
                            axis=0).T
    qs = q * MOBA_SCALE
    qb = qs.astype(BF16)
    q_aug = (qs + flags).astype(BF16)

    _online_init(m_sc, l_sc, acc_sc)
    chunks_per_head = tq // ROW_CHUNK
    q_in = _iota((ROW_CHUNK, tq), 0)
    col = _iota((ROW_CHUNK, tq), 1)
    kv_own = kv_ref[pl.ds(pl.multiple_of(i * tq, tq), tq), :].astype(BF16)
    _flash_update(m_sc, l_sc, acc_sc, qb, kv_own, kv_own,
                  lambda c: col <= q_in + (c % chunks_per_head) * ROW_CHUNK)

    def past_blocks(first, count):
        kv_j = kv_ref[pl.ds(pl.multiple_of(first * tq, tq), count * tq), :]
        key_block = first + (_iota((count * tq, LANES), 0) >> (tq.bit_length() - 1))
        _flash_update(m_sc, l_sc, acc_sc, q_aug, _block_bias_keys(kv_j, key_block), kv_j.astype(BF16))

    lax.fori_loop(0, i // 2, lambda jj, carry: past_blocks(2 * jj, 2) or carry, 0)

    @pl.when(i % 2 == 1)
    def _():
        past_blocks(i - 1, 1)
    o = _online_finish(l_sc, acc_sc)
    for h in range(MOBA_HPG):
        o_ref[:, h * LANES:(h + 1) * LANES] = o[h * tq:(h + 1) * tq, :]


def moba_prompt(mq, mkv):
    n, t, _ = mq.shape
    n_blocks = t // MOBA_BLOCK
    rows = MOBA_HPG * MOBA_BLOCK
    gw = MOBA_HPG * LANES
    return pl.pallas_call(
        functools.partial(_moba_prompt_kernel, n_blocks=n_blocks),
        grid=(n, MOBA_KV_HEADS, n_blocks),
        in_specs=[pl.BlockSpec((None, MOBA_BLOCK, gw), lambda b, g, i: (b, i, g)),
                  pl.BlockSpec((None, t, LANES), lambda b, g, i: (b, 0, g))],
        out_specs=pl.BlockSpec((None, MOBA_BLOCK, gw), lambda b, g, i: (b, i, g)),
        out_shape=jax.ShapeDtypeStruct((n, t, HEADS * LANES), F32),
        scratch_shapes=[pltpu.VMEM((rows, LANES), F32), pltpu.VMEM((rows, LANES), F32), pltpu.VMEM((rows, LANES), F32)],
        compiler_params=_params(3),
        name="moba_prompt",
    )(mq, mkv)


def _compress(load_rows, n_chunks, pe_ref, w1_ref, w2_ref):
    lo = jnp.zeros((n_chunks, 2 * NSA_PHI_HIDDEN), F32)
    hi = jnp.zeros((n_chunks, 2 * NSA_PHI_HIDDEN), F32)
    half = NSA_CMP_STRIDE // 2
    for j in range(half):
        x = jnp.concatenate([load_rows(2 * j), load_rows(2 * j + 1)], axis=1)
        lo = lo + _mm(x + pe_ref[j:j + 1, :], w1_ref[j])
        hi = hi + _mm(x + pe_ref[half + j:half + j + 1, :], w1_ref[half + j])
    hidden = lo + pltpu.roll(hi, n_chunks - 1, 0)
    return _mm(hidden * _sigmoid(hidden), w2_ref[...])


def _nsa_compress_kernel(rows_ref, pe_ref, w1_ref, w2_ref, out_ref, *, n_chunks):
    out_ref[...] = _compress(lambda r: rows_ref[pl.ds(r, n_chunks, stride=NSA_CMP_STRIDE), :],
                             n_chunks, pe_ref, w1_ref, w2_ref)


def nsa_compress_prompt(cmp_rows, pe, w1, w2):
    n, t, _ = cmp_rows.shape
    n_chunks = t // NSA_CMP_STRIDE
    return pl.pallas_call(
        functools.partial(_nsa_compress_kernel, n_chunks=n_chunks),
        grid=(n,),
        in_specs=[pl.BlockSpec((None, t, LANES), lambda b: (b, 0, 0)), _const_spec(pe.shape),
                  _const_spec(w1.shape), _const_spec(w2.shape)],
        out_specs=pl.BlockSpec((None, n_chunks, LANES), lambda b: (b, 0, 0)),
        out_shape=jax.ShapeDtypeStruct((n, n_chunks, LANES), F32),
        compiler_params=_params(1),
        name="nsa_compress",
    )(cmp_rows, pe, w1, w2)


def _nsa_select(p_sum, q_pos, ovl_ref, rows_q, width, n_sel_blocks):
    imp = _mm(p_sum, ovl_ref[...])
    blk = _iota((rows_q, width), 1)
    cur = q_pos >> SEL_SHIFT
    forced = (blk == 0) | (blk == cur) | (blk == cur - 1)
    avail = (blk * NSA_SEL_BLOCK <= q_pos) & (blk < n_sel_blocks)
    score = jnp.where(avail, jnp.where(forced, SEL_FORCE, imp), NEG_INF)
    return avail & (_rank_before(score, n_sel_blocks) < NSA_SEL_TOPN)


def _nsa_prompt_kernel(qraw_ref, qrot_ref, gate_ref, ckcv_ref, rows_ref, win_ref, ovl_ref, o_ref,
                       m_sc, l_sc, acc_sc, *, tq, tk, n_cmp, n_sel_blocks):
    i = pl.program_id(1)
    nc_pad = ckcv_ref.shape[0]
    gate = gate_ref[...]
    q_raw = (_stack_heads(qraw_ref, LANES) * NSA_SCALE).astype(BF16)
    q_rot = _stack_heads(qrot_ref, LANES) * NSA_SCALE

    ckcv = ckcv_ref[...].astype(BF16)
    c_idx = _iota((tq, nc_pad), 1)
    vis_c = (c_idx * NSA_CMP_STRIDE + (NSA_CMP_LEN - 1) <= i * tq + _iota((tq, nc_pad), 0)) & (c_idx < n_cmp)
    o_c, p_c = _attend_once(q_raw, ckcv, ckcv, vis_c)
    p_sum = p_c[0]
    for h in range(1, HEADS):
        p_sum = p_sum + p_c[h]

    width = ovl_ref.shape[1]
    blk = _iota((tq, width), 1)
    sel = _nsa_select(p_sum, i * tq + _iota((tq, width), 0), ovl_ref, tq, width, n_sel_blocks)
    flags = pltpu.roll(jnp.where((blk < n_sel_blocks) & ~sel, 1.0, 0.0), DH, 1)
    q_aug = (q_rot + _tile_rows(flags, HEADS)).astype(BF16)
    _online_init(m_sc, l_sc, acc_sc)

    def tile(t, vis):
        kv = rows_ref[pl.ds(pl.multiple_of(t * tk, tk), tk), :]
        key_block = (t * tk + _iota((tk, LANES), 0)) >> SEL_SHIFT
        _flash_update(m_sc, l_sc, acc_sc, q_aug, _block_bias_keys(kv, key_block), kv.astype(BF16), vis)

    diag = (i * tq) // tk
    causal = diag * tk + _iota((tq, tk), 1) <= i * tq + _iota((tq, tk), 0)
    tile(diag, lambda c: causal)
    lax.fori_loop(0, diag, lambda t, carry: tile(t, None) or carry, 0)
    o_s = _online_finish(l_sc, acc_sc)

    slab = NSA_WINDOW + tq
    start = jnp.maximum(i * tq - NSA_WINDOW, 0)
    kv_w = win_ref[pl.ds(pl.multiple_of(start, tq), slab), :].astype(BF16)
    dist = i * tq + _iota((tq, slab), 0) - (start + _iota((tq, slab), 1))
    o_w, _ = _attend_once(q_rot.astype(BF16), kv_w, kv_w, (dist >= 0) & (dist <= NSA_WINDOW))

    for h in range(HEADS):
        r = slice(h * tq, (h + 1) * tq)
        o_ref[:, h * LANES:(h + 1) * LANES] = (gate[:, h:h + 1] * o_c[r] + gate[:, HEADS + h:HEADS + h + 1] * o_s[r]
                                               + gate[:, 2 * HEADS + h:2 * HEADS + h + 1] * o_w[r])


def _overlap_matrix(nc_pad, n_cmp, width):
    start = np.arange(nc_pad)[:, None] * NSA_CMP_STRIDE
    j0 = np.arange(width)[None, :] * NSA_SEL_BLOCK
    ovl = (start < j0 + NSA_SEL_BLOCK) & (start + NSA_CMP_LEN > j0) & (np.arange(nc_pad)[:, None] < n_cmp)
    return jnp.asarray(ovl, BF16)


def nsa_prompt(nq_raw, nq_rot, gate, ckcv, slc_rows, win_rows, *, tq=ROW_CHUNK, tk=512):
    n, t, _ = nq_raw.shape
    tk = min(tk, t)
    nc_pad = t // NSA_CMP_STRIDE
    n_cmp = (t - NSA_CMP_LEN) // NSA_CMP_STRIDE + 1
    n_sel_blocks = t // NSA_SEL_BLOCK
    assert tq == ROW_CHUNK and tk % tq == 0 and n_sel_blocks <= DH
    assert t >= NSA_WINDOW + tq and NSA_WINDOW % tq == 0
    width = LANES
    ovl = _overlap_matrix(nc_pad, n_cmp, width)
    rows = HEADS * tq
    qspec = pl.BlockSpec((None, tq, HEADS * LANES), lambda b, i: (b, i, 0))
    return pl.pallas_call(
        functools.partial(_nsa_prompt_kernel, tq=tq, tk=tk, n_cmp=n_cmp, n_sel_blocks=n_sel_blocks),
        grid=(n, t // tq),
        in_specs=[qspec, qspec, pl.BlockSpec((None, tq, LANES), lambda b, i: (b, i, 0)),
                  pl.BlockSpec((None, nc_pad, LANES), lambda b, i: (b, 0, 0)),
                  pl.BlockSpec((None, t, LANES), lambda b, i: (b, 0, 0)),
                  pl.BlockSpec((None, t, LANES), lambda b, i: (b, 0, 0)),
                  _const_spec(ovl.shape)],
        out_specs=qspec,
        out_shape=jax.ShapeDtypeStruct((n, t, HEADS * LANES), F32),
        scratch_shapes=[pltpu.VMEM((rows, LANES), F32), pltpu.VMEM((rows, LANES), F32), pltpu.VMEM((rows, LANES), F32)],
        compiler_params=_params(2),
        name="nsa_prompt",
    )(nq_raw, nq_rot, gate, ckcv, slc_rows, win_rows, ovl)


def _pad_rows(x, n):
    return jnp.concatenate([x, jnp.zeros((n - x.shape[0], x.shape[1]), x.dtype)], axis=0)


def _all_pages(pages, lo, hi):
    return jnp.concatenate([pg[lo:hi, :] for pg in pages], axis=1).astype(BF16)


def _block_bias_matrix(width, n_blocks, block, n_tokens):
    blk = np.arange(width)[:, None]
    tok = np.arange(n_tokens)[None, :]
    return jnp.asarray(np.where((tok // block == blk) & (blk < n_blocks), NEG_INF, 0.0), BF16)


def _paged_attend(s_past, v_t, s_new, vis_new, v_new, bias=None):
    ps, ps_new = [], []
    for c in range(s_past.shape[0] // 8):
        r = slice(8 * c, 8 * c + 8)
        sp, vn = s_past[r] if bias is None else s_past[r] + bias, vis_new[r]
        sn = jnp.where(vn, s_new[r], NEG_INF)
        m = jnp.maximum(jnp.max(sp, axis=1, keepdims=True), jnp.max(sn, axis=1, keepdims=True))
        e, en = jnp.exp(sp - m), jnp.where(vn, jnp.exp(sn - m), 0.0)
        inv = 1.0 / jnp.maximum(jnp.sum(e, axis=1, keepdims=True) + jnp.sum(en, axis=1, keepdims=True), 1e-30)
        ps.append((e * inv).astype(BF16))
        ps_new.append((en * inv).astype(BF16))
    return _mm_nt(jnp.concatenate(ps, axis=0), v_t) + _mm(jnp.concatenate(ps_new, axis=0), v_new)


def _new_token_vis(rows, s_len):
    return _iota((rows, LANES), 1) <= (_iota((rows, LANES), 0) & (s_len - 1))


SEQ_PER_STEP = 1
NSA_SEQ_PER_STEP = 2


def _page_specs(sps, n_pages, rows, layer):
    return [pl.BlockSpec((None, None, rows, LANES),
                         functools.partial(lambda i, pt, u, p: (layer, pt[sps * i + u, p], 0, 0), u=u, p=p))
            for u in range(sps) for p in range(n_pages)]


def _seq_spec(sps, s_len, width):
    return pl.BlockSpec((sps, s_len, width), lambda i, pt: (i, 0, 0))


def _per_sequence(body, sps, n_seq_in, n_const, n_pages):
    def kernel(pt_ref, *refs):
        seq_in, refs = refs[:n_seq_in], refs[n_seq_in:]
        consts, refs = refs[:n_const], refs[n_const:]
        pages, refs = refs[:sps * n_pages], refs[sps * n_pages:]
        for u in range(sps):
            body(pt_ref, *[r.at[u] for r in seq_in], *consts, *pages[u * n_pages:(u + 1) * n_pages],
                 *[r.at[u] for r in refs])
    return kernel


def _moba_sample_kernel(pt_ref, q_ref, kvn_ref, bias_ref, *rest, n_pages, s_len, page):
    pages, (o_ref,) = rest[:n_pages], rest[n_pages:]
    rows = HEADS * s_len
    qb = (_stack_heads(q_ref, LANES) * MOBA_SCALE).astype(BF16)
    n_blocks = n_pages * page // MOBA_BLOCK
    s_raw = _mm(qb, _all_pages(pages, 0, LANES))
    lane = _iota((rows, LANES), 1)
    s_gate = jnp.full((rows, LANES), NEG_INF, F32)
    for blk in range(n_blocks):
        tot = jnp.sum(s_raw[:, blk * MOBA_BLOCK:(blk + 1) * MOBA_BLOCK], axis=1, keepdims=True)
        s_gate = jnp.where(lane == blk, tot * (1.0 / MOBA_BLOCK), s_gate)
    not_sel = jnp.where((lane < n_blocks) & ~(_rank_before(s_gate, n_blocks) < MOBA_TOPK), 1.0, 0.0)

    kv_new = _pad_rows(kvn_ref[...], LANES).astype(BF16)
    o = _paged_attend(s_raw + _mm(not_sel, bias_ref[...]), _all_pages(pages, LANES, 2 * LANES),
                      _mm_nt(qb, kv_new[:, 0:LANES]), _new_token_vis(rows, s_len), kv_new[:, LANES:2 * LANES])
    _unstack_heads(o_ref, o, s_len)


def moba_sample(mq, mkv_new, cache_t, page_table, *, layer):
    b, s_len, _ = mq.shape
    n_pages = page_table.shape[1]
    page = cache_t.shape[3]
    assert (n_pages * page) % MOBA_BLOCK == 0 and MOBA_BLOCK % page == 0 and s_len <= LANES
    n_blocks = n_pages * page // MOBA_BLOCK
    assert MOBA_TOPK <= n_blocks <= LANES
    bias = _block_bias_matrix(LANES, n_blocks, MOBA_BLOCK, n_pages * page)
    sps = SEQ_PER_STEP
    assert b % sps == 0
    body = functools.partial(_moba_sample_kernel, n_pages=n_pages, s_len=s_len, page=page)
    return pl.pallas_call(
        _per_sequence(body, sps, 2, 1, n_pages),
        grid_spec=pltpu.PrefetchScalarGridSpec(
            num_scalar_prefetch=1, grid=(b // sps,),
            in_specs=[_seq_spec(sps, s_len, HEADS * LANES), _seq_spec(sps, s_len, 256),
                      pl.BlockSpec(bias.shape, lambda i, pt: (0, 0))] + _page_specs(sps, n_pages, 256, layer),
            out_specs=_seq_spec(sps, s_len, HEADS * LANES)),
        out_shape=jax.ShapeDtypeStruct((b, s_len, HEADS * LANES), F32),
        compiler_params=_params(1),
        name="moba_sample",
    )(page_table, mq, mkv_new, bias, *([cache_t] * (sps * n_pages)))


def _mla_sample_kernel(pt_ref, q_ref, kn_ref, *rest, n_pages, s_len, page):
    pages, (o_ref,) = rest[:n_pages], rest[n_pages:]
    rows = HEADS * s_len
    qb = (_stack_heads(q_ref, MLA_QW) * MLA_SCALE).astype(BF16)
    k_new = _pad_rows(kn_ref[...], LANES).astype(BF16)
    rows_t = _all_pages(pages, 0, MLA_ROW)
    o = _paged_attend(_mm(qb[:, 0:MLA_ROW], rows_t), rows_t[0:MLA_KV_LORA, :], _mm_nt(qb, k_new),
                      _new_token_vis(rows, s_len), k_new[:, 0:MLA_KV_LORA])
    _unstack_heads(o_ref, o, s_len)


def mla_sample(qcat, kcat_new, cache_t, page_table, *, layer):
    b, s_len, _ = qcat.shape
    n_pages = page_table.shape[1]
    page = cache_t.shape[3]
    sps = SEQ_PER_STEP
    assert page == LANES and s_len <= LANES and b % sps == 0
    body = functools.partial(_mla_sample_kernel, n_pages=n_pages, s_len=s_len, page=page)
    return pl.pallas_call(
        _per_sequence(body, sps, 2, 0, n_pages),
        grid_spec=pltpu.PrefetchScalarGridSpec(
            num_scalar_prefetch=1, grid=(b // sps,),
            in_specs=[_seq_spec(sps, s_len, HEADS * MLA_QW), _seq_spec(sps, s_len, MLA_QW)]
            + _page_specs(sps, n_pages, MLA_ROW, layer),
            out_specs=_seq_spec(sps, s_len, HEADS * LANES)),
        out_shape=jax.ShapeDtypeStruct((b, s_len, HEADS * LANES), F32),
        compiler_params=_params(1),
        name="mla_sample",
    )(page_table, qcat, kcat_new, *([cache_t] * (sps * n_pages)))


def _nsa_sample_kernel(pt_ref, qraw_ref, qrot_ref, gate_ref, slcn_ref, winn_ref, win_ref, pe_ref, w1_ref, w2_ref,
                       ovl_ref, *rest, n_pages, s_len, page, past_len):
    pages, (o_ref, xs_sc) = rest[:n_pages], rest[n_pages:]
    rows = HEADS * s_len
    q_raw = (_stack_heads(qraw_ref, LANES) * NSA_SCALE).astype(BF16)
    q_rot = (_stack_heads(qrot_ref, LANES) * NSA_SCALE).astype(BF16)

    for p in range(n_pages):
        xs_sc[p * page:(p + 1) * page, :] = pages[p][0:LANES, :].T
    n_chunks = past_len // NSA_CMP_STRIDE
    n_cmp = n_chunks - 1
    ckcv = _compress(lambda r: xs_sc[pl.ds(r, n_chunks, stride=NSA_CMP_STRIDE), :],
                     n_chunks, pe_ref, w1_ref, w2_ref).astype(BF16)
    q_pos_c = past_len + (_iota((rows, n_chunks), 0) & (s_len - 1))
    c_idx = _iota((rows, n_chunks), 1)
    vis_c = (c_idx * NSA_CMP_STRIDE + (NSA_CMP_LEN - 1) <= q_pos_c) & (c_idx < n_cmp)
    o_c, (p_c,) = _softmax_parts([(_mm_nt(q_raw, ckcv), vis_c, ckcv, False)])

    width = ovl_ref.shape[1]
    n_sel_blocks = -(-(past_len + s_len) // NSA_SEL_BLOCK)
    q_pos1 = past_len + _iota((s_len, width), 0)
    p_sum = p_c[0:s_len, :]
    for h in range(1, HEADS):
        p_sum = p_sum + p_c[h * s_len:(h + 1) * s_len, :]
    sel = _nsa_select(p_sum, q_pos1, ovl_ref, s_len, width, n_sel_blocks)
    hide = jnp.where(sel, 0.0, NEG_INF)
    bpp = page // NSA_SEL_BLOCK
    blk_in_page = _iota((s_len, page), 1) >> SEL_SHIFT
    pieces = []
    for p in range(n_pages):
        piece = jnp.broadcast_to(hide[:, p * bpp:p * bpp + 1], (s_len, page))
        for u in range(1, bpp):
            piece = jnp.where(blk_in_page == u, jnp.broadcast_to(hide[:, p * bpp + u:p * bpp + u + 1], (s_len, page)),
                              piece)
        pieces.append(piece)
    slc_t = _all_pages(pages, LANES, 2 * LANES)
    new_blk = past_len // NSA_SEL_BLOCK
    sel_new = jnp.where(sel[:, new_blk:new_blk + 1], 1.0, 0.0)
    vis_new = _new_token_vis(rows, s_len) & (_tile_rows(jnp.broadcast_to(sel_new, (s_len, LANES)), HEADS) > 0.5)
    slc_new = _pad_rows(slcn_ref[...], LANES).astype(BF16)
    o_s = _paged_attend(_mm(q_rot, slc_t), slc_t, _mm_nt(q_rot, slc_new), vis_new, slc_new,
                        bias=jnp.concatenate(pieces, axis=1))

    wb = win_ref.shape[1]
    q_pos_w = past_len + (_iota((rows, wb), 0) & (s_len - 1))
    w_pos = past_len - wb + _iota((rows, wb), 1)
    dist = q_pos_w - w_pos
    vis_buf = (dist >= 0) & (dist <= NSA_WINDOW) & (w_pos >= 0)
    win_t = win_ref[...].astype(BF16)
    win_new = _pad_rows(winn_ref[...], LANES).astype(BF16)
    o_w, _ = _softmax_parts([(_mm(q_rot, win_t), vis_buf, win_t, True),
                             (_mm_nt(q_rot, win_new), _new_token_vis(rows, s_len), win_new, False)])

    gate = gate_ref[...]
    for h in range(HEADS):
        sl = slice(h * s_len, (h + 1) * s_len)
        o_ref[:, h * LANES:(h + 1) * LANES] = (gate[:, h:h + 1] * o_c[sl]
                                               + gate[:, HEADS + h:HEADS + h + 1] * o_s[sl]
                                               + gate[:, 2 * HEADS + h:2 * HEADS + h + 1] * o_w[sl])


def nsa_sample(nq_raw, nq_rot, gate, slc_new, win_new, cache_t, win_t, page_table, pe, w1, w2, *, layer):
    b, s_len, _ = nq_raw.shape
    n_pages = page_table.shape[1]
    page = cache_t.shape[3]
    past_len = n_pages * page
    wb = win_t.shape[3]
    assert page == LANES and page % NSA_SEL_BLOCK == 0 and past_len % NSA_CMP_STRIDE == 0
    assert s_len < NSA_CMP_STRIDE and s_len <= NSA_SEL_BLOCK and wb <= past_len and NSA_WINDOW <= wb + s_len
    n_chunks = past_len // NSA_CMP_STRIDE
    n_sel_blocks = -(-(past_len + s_len) // NSA_SEL_BLOCK)
    width = -(-n_sel_blocks // LANES) * LANES
    sps = NSA_SEQ_PER_STEP if b % NSA_SEQ_PER_STEP == 0 else 1
    assert s_len == 8
    ovl = _overlap_matrix(n_chunks, n_chunks - 1, width)
    const = lambda a: pl.BlockSpec(a.shape, lambda i, pt: (0,) * a.ndim)
    body = functools.partial(_nsa_sample_kernel, n_pages=n_pages, s_len=s_len, page=page, past_len=past_len)
    return pl.pallas_call(
        _per_sequence(body, sps, 6, 4, n_pages),
        grid_spec=pltpu.PrefetchScalarGridSpec(
            num_scalar_prefetch=1, grid=(b // sps,),
            in_specs=[_seq_spec(sps, s_len, HEADS * LANES), _seq_spec(sps, s_len, HEADS * LANES),
                      _seq_spec(sps, s_len, LANES), _seq_spec(sps, s_len, LANES), _seq_spec(sps, s_len, LANES),
                      pl.BlockSpec((None, sps, LANES, wb), lambda i, pt: (layer, i, 0, 0)),
                      const(pe), const(w1), const(w2), const(ovl)] + _page_specs(sps, n_pages, 256, layer),
            out_specs=_seq_spec(sps, s_len, HEADS * LANES),
            scratch_shapes=[pltpu.VMEM((sps, past_len, LANES), F32)]),
        out_shape=jax.ShapeDtypeStruct((b, s_len, HEADS * LANES), F32),
        compiler_params=_params(1),
        name="nsa_sample",
    )(page_table, nq_raw, nq_rot, gate, slc_new, win_new, win_t, pe, w1, w2, ovl, *([cache_t] * (sps * n_pages)))


def _w_all_columns(q_lane_offset, kv_group_major):
    src = np.full(W_ALL_COLS, -1, np.int64)
    for h in range(HEADS):
        src[G_MQ + h * LANES + q_lane_offset(h) + np.arange(DH)] = S_MQ + h * DH + np.arange(DH)
        src[G_NQ + h * LANES + np.arange(DH)] = S_NQ + h * DH + np.arange(DH)
    for g in range(MOBA_KV_HEADS):
        k_col = g * LANES if kv_group_major else g * DH
        v_col = g * LANES + DH if kv_group_major else LANES + g * DH
        src[G_MKV + k_col + np.arange(DH)] = S_MK + g * DH + np.arange(DH)
        src[G_MKV + v_col + np.arange(DH)] = S_MV + g * DH + np.arange(DH)
    src[G_CQ:G_CQ + MLA_Q_LORA] = S_CQ + np.arange(MLA_Q_LORA)
    src[G_CKV:G_CKV + MLA_KV_LORA] = S_CKV + np.arange(MLA_KV_LORA)
    src[G_KR:G_KR + MLA_ROPE] = S_KR + np.arange(MLA_ROPE)
    src[G_NSA:G_NSA + 4 * DH] = S_KC + np.arange(4 * DH)
    src[G_WIN:G_WIN + 2 * DH] = S_KW + np.arange(2 * DH)
    src[G_NG:G_NG + 3 * HEADS] = S_NG + np.arange(3 * HEADS)
    return src


def _take_static(w, src, axis):
    axis = axis % w.ndim
    pieces, i, n = [], 0, len(src)
    while i < n:
        j = i + 1
        if src[i] < 0:
            while j < n and src[j] < 0:
                j += 1
            pieces.append(jnp.zeros(w.shape[:axis] + (j - i,) + w.shape[axis + 1:], w.dtype))
        else:
            while j < n and src[j] == src[j - 1] + 1:
                j += 1
            pieces.append(lax.slice_in_dim(w, int(src[i]), int(src[i]) + (j - i), axis=axis))
        i = j
    return jnp.concatenate(pieces, axis=axis)


def _take_cols(w, src):
    return _take_static(w, src, -1)


def _pad_head_rows(w, lane_offset):
    src = np.full(HEADS * LANES, -1, np.int64)
    for h in range(HEADS):
        src[h * LANES + lane_offset(h) + np.arange(DH)] = h * DH + np.arange(DH)
    return _take_static(w, src, 1)


def _block_diag_heads(w):
    l, h, a, b = w.shape
    eye = jnp.eye(h, dtype=w.dtype)
    return (w[:, :, :, None, :] * eye[None, :, None, :, None]).reshape(l, h * a, h * b)


def _mla_query_weights(w_uq, w_uk):
    src = np.zeros(768, np.int64)
    per = MLA_NOPE + MLA_ROPE
    for h in range(HEADS):
        src[h * MLA_NOPE + np.arange(MLA_NOPE)] = h * per + np.arange(MLA_NOPE)
        src[512 + h * MLA_ROPE + np.arange(MLA_ROPE)] = h * per + MLA_NOPE + np.arange(MLA_ROPE)
    w_uq_all = _take_cols(w_uq, src)
    depth = w_uk.shape[0]
    wk = w_uk.reshape(depth, MLA_KV_LORA, HEADS, MLA_NOPE).transpose(0, 2, 3, 1)
    wk = jnp.pad(wk, ((0, 0), (0, 0), (0, 0), (0, MLA_QW - MLA_KV_LORA)))
    top = _block_diag_heads(wk)
    bottom = np.zeros((HEADS * MLA_ROPE, HEADS * MLA_QW), np.float32)
    for h in range(HEADS):
        for r in range(MLA_ROPE):
            bottom[h * MLA_ROPE + r, h * MLA_QW + MLA_KV_LORA + r] = 1.0
    w_cat = jnp.concatenate([top, jnp.broadcast_to(jnp.asarray(bottom), (depth,) + bottom.shape)], axis=1)
    return w_uq_all.astype(BF16), w_cat.astype(BF16)


def _compress_weights(pe_k, w1_k, w2_k, pe_v, w1_v, w2_v):
    depth = pe_k.shape[0]
    pe = jnp.concatenate([pe_k, pe_v], axis=-1)
    w1 = jnp.stack([w1_k.reshape(depth, NSA_CMP_LEN, DH, NSA_PHI_HIDDEN),
                    w1_v.reshape(depth, NSA_CMP_LEN, DH, NSA_PHI_HIDDEN)], axis=2)
    w1 = _block_diag_heads(w1.reshape(depth * NSA_CMP_LEN, 2, DH, NSA_PHI_HIDDEN))
    w1 = w1.reshape(depth, NSA_CMP_LEN // 2, 4 * DH, 2 * NSA_PHI_HIDDEN)
    pe = pe.reshape(depth, NSA_CMP_LEN // 2, 4 * DH)
    w2 = _block_diag_heads(jnp.stack([w2_k, w2_v], axis=1))
    return pe, w1.astype(BF16), w2.astype(BF16)


def _rope_tables(pos):
    posf = pos.astype(F32)[:, None]
    lane = np.arange(LANES)

    def tables(width, rot):
        d = lane % width
        half = rot // 2
        inv_freq = ROPE_THETA ** (-2.0 * jnp.arange(half, dtype=F32) / rot)
        ang = posf * inv_freq[jnp.asarray(d % half)]
        cos, sin = jnp.cos(ang), jnp.sin(ang)
        c = jnp.where(jnp.asarray(d < rot), cos, 1.0)
        s = jnp.where(jnp.asarray(d < half), -sin, jnp.where(jnp.asarray(d < rot), sin, 0.0))
        return c, s

    ca, sa = tables(DH, ROT16)
    cb, sb = tables(MLA_ROPE, MLA_ROPE)
    return ca, sa, cb, sb


def _forward(cfg, x_prompt, x_sample, cache_moba, cache_mla, cache_nsa, state_nsa_win, page_table,
             norm_mix, w_in, mla_q_norm, mla_w_uq, mla_kv_norm, mla_w_uk, mla_w_uv,
             nsa_pe_k, nsa_w1_k, nsa_w2_k, nsa_pe_v, nsa_w1_v, nsa_w2_v,
             w_oa, w_ob, w_oc, w_gate, w_out, norm_ffn, w_ff1, w_ff2, norm_final):
    n, t, b, s_len = cfg.n_seq, cfg.t_len, cfg.dec_batch, cfg.dec_seq
    depth = w_in.shape[0]
    mp, ms = n * t, b * s_len
    tm_p, tm_s = 256, min(256, ms)
    bf = lambda w: w.astype(BF16)
    vec = lambda g: g.reshape(depth, 1, -1)

    value_half = lambda h: DH
    group_lane = lambda h: DH * (h // MOBA_HPG)
    w_all_p = bf(_take_cols(w_in, _w_all_columns(lambda h: 0, True)))
    w_all_s = bf(_take_cols(w_in, _w_all_columns(group_lane, False)))
    w_uq_all, w_cat = _mla_query_weights(mla_w_uq, mla_w_uk)
    w_uv_bd = bf(_block_diag_heads(mla_w_uv.reshape(depth, MLA_KV_LORA, HEADS, DH).transpose(0, 2, 1, 3)))
    w_oa_p, w_oa_s = bf(_pad_head_rows(w_oa, value_half)), bf(_pad_head_rows(w_oa, group_lane))
    w_oc_pad = bf(_pad_head_rows(w_oc, value_half))
    pe, w1c, w2c = _compress_weights(nsa_pe_k, nsa_w1_k, nsa_w2_k, nsa_pe_v, nsa_w1_v, nsa_w2_v)
    w_ob_b, w_gate_b, w_out_b, w_ff1_b, w_ff2_b = bf(w_ob), bf(w_gate), bf(w_out), bf(w_ff1), bf(w_ff2)
    g_mix, g_ffn, g_q, g_kv = vec(norm_mix), vec(norm_ffn), vec(mla_q_norm), vec(mla_kv_norm)
    g_final = norm_final.reshape(1, -1)

    tabs_p = _rope_tables(jnp.arange(t))
    tabs_s = _rope_tables(cfg.past_len + (jnp.arange(tm_s) % s_len))

    pool = cache_moba.shape[1]
    cm_t = cache_moba.transpose(0, 1, 3, 4, 5, 2).reshape(depth, pool, 4 * DH, cfg.page)
    cn_t = cache_nsa.transpose(0, 1, 3, 4, 2).reshape(depth, pool, 4 * DH, cfg.page)
    cl_t = cache_mla.transpose(0, 1, 3, 2)
    win_t = state_nsa_win.transpose(0, 1, 3, 4, 2).reshape(depth, b, 2 * DH, -1)

    xp = x_prompt.reshape(mp, D_MODEL)
    xs = x_sample.reshape(ms, D_MODEL)
    leaves = [[] for _ in range(8)]
    for l in range(depth):
        last = l == depth - 1
        mq, mkv, qcat, kcat, nq_raw, nq_rot, cmp_r, slc_r, win_r, gate = in_proj(
            xp, g_mix[l], w_all_p[l], g_q[l], w_uq_all[l], w_cat[l], g_kv[l], tabs_p, tm=tm_p, kv_group_major=True)
        r3 = lambda a: a.reshape(n, t, -1)
        o_a = moba_prompt(r3(mq), r3(mkv))
        o_b = mla_prompt(r3(qcat), r3(kcat))
        ckcv = nsa_compress_prompt(r3(cmp_r), pe[l], w1c[l], w2c[l])
        o_c = nsa_prompt(r3(nq_raw), r3(nq_rot), r3(gate), ckcv, r3(slc_r), r3(win_r))
        x1 = merge(xp, g_mix[l], w_gate_b[l], o_a.reshape(mp, -1), o_b.reshape(mp, -1), o_c.reshape(mp, -1),
                   w_uv_bd[l], w_oa_p[l], w_ob_b[l], w_oc_pad[l], w_out_b[l], tm=256)
        xp = ffn(x1, g_ffn[l], w_ff1_b[l], w_ff2_b[l], g_final, tm=256, final_norm=last)
        leaves[0].append(mkv.reshape(n, t, MOBA_KV_HEADS, 2, DH).transpose(0, 1, 3, 2, 4))
        leaves[2].append(kcat[:, :MLA_ROW].reshape(n, t, MLA_ROW))
        leaves[4].append(jnp.concatenate([cmp_r, slc_r], axis=1).reshape(n, t, 4, DH))
        leaves[6].append(win_r.reshape(n, t, 2, DH)[:, t - min(NSA_WINDOW, t):])
        mq, mkv, qcat, kcat, nq_raw, nq_rot, cmp_r, slc_r, win_r, gate = in_proj(
            xs, g_mix[l], w_all_s[l], g_q[l], w_uq_all[l], w_cat[l], g_kv[l], tabs_s, tm=tm_s, kv_group_major=False)
        r3 = lambda a: a.reshape(b, s_len, -1)
        o_a = moba_sample(r3(mq), r3(mkv), cm_t, page_table, layer=l)
        o_b = mla_sample(r3(qcat), r3(kcat), cl_t, page_table, layer=l)
        o_c = nsa_sample(r3(nq_raw), r3(nq_rot), r3(gate), r3(slc_r), r3(win_r), cn_t, win_t, page_table,
                         pe[l], w1c[l], w2c[l], layer=l)
        x1 = merge(xs, g_mix[l], w_gate_b[l], o_a.reshape(ms, -1), o_b.reshape(ms, -1), o_c.reshape(ms, -1),
                   w_uv_bd[l], w_oa_s[l], w_ob_b[l], w_oc_pad[l], w_out_b[l], tm=min(256, ms))
        xs = ffn(x1, g_ffn[l], w_ff1_b[l], w_ff2_b[l], g_final, tm=min(256, ms), final_norm=last)
        leaves[1].append(mkv.reshape(b, s_len, 2, MOBA_KV_HEADS, DH))
        leaves[3].append(kcat[:, :MLA_ROW].reshape(b, s_len, MLA_ROW))
        leaves[5].append(jnp.concatenate([cmp_r, slc_r], axis=1).reshape(b, s_len, 4, DH))
        win_full = jnp.concatenate([state_nsa_win[l], win_r.reshape(b, s_len, 2, DH)], axis=1)
        leaves[7].append(win_full[:, win_full.shape[1] - min(NSA_WINDOW, win_full.shape[1]):])
    return (xp.reshape(n, t, D_MODEL), xs.reshape(b, s_len, D_MODEL)) + tuple(jnp.stack(v) for v in leaves)


def kernel(x_prompt, x_sample, cache_moba, cache_mla, cache_nsa, state_nsa_win, page_table, norm_mix, w_in,
           mla_q_norm, mla_w_uq, mla_kv_norm, mla_w_uk, mla_w_uv, nsa_pe_k, nsa_w1_k, nsa_w2_k, nsa_pe_v,
           nsa_w1_v, nsa_w2_v, w_oa, w_ob, w_oc, w_gate, w_out, norm_ffn, w_ff1, w_ff2, norm_final):
    return _forward(PROD, x_prompt, x_sample, cache_moba, cache_mla, cache_nsa, state_nsa_win, page_table,
                    norm_mix, w_in, mla_q_norm, mla_w_uq, mla_kv_norm, mla_w_uk, mla_w_uv,
                    nsa_pe_k, nsa_w1_k, nsa_w2_k, nsa_pe_v, nsa_w1_v, nsa_w2_v,
                    w_oa, w_ob, w_oc, w_gate, w_out, norm_ffn, w_ff1, w_ff2, norm_final)
```

```python
import collections
import functools

import numpy as np
import jax
import jax.numpy as jnp
from jax import lax
from jax.experimental import pallas as pl
from jax.experimental.pallas import tpu as pltpu

F32 = jnp.float32
BF16 = jnp.bfloat16

D_MODEL = 1024
BATCH = 4
SEQ = 4096
DEPTH = 4
DEC_BATCH = 128
DEC_SEQ = 8
PAST_LEN = 8192
PAGE_SIZE = 128

ROPE_THETA = 500000.0
NORM_EPS = 1e-6
NEG_INF = -1e30
SEL_FORCE = 1e30
LOG2E = 1.4426950408889634

HEADS = 8
DH = 64
LANES = 128
ROW_CHUNK = 128
MOBA_KV_HEADS = 2
MOBA_HPG = HEADS // MOBA_KV_HEADS
MOBA_BLOCK = 256
MOBA_TOPK = 3
MOBA_SCALE = DH ** -0.5
ROT16 = 16

MLA_NOPE = 64
MLA_ROPE = 32
MLA_Q_LORA = 256
MLA_KV_LORA = 128
MLA_ROW = MLA_KV_LORA + MLA_ROPE
MLA_QW = 256
MLA_SCALE = (MLA_NOPE + MLA_ROPE) ** -0.5

NSA_CMP_LEN = 32
NSA_CMP_STRIDE = 16
NSA_SEL_BLOCK = 64
SEL_SHIFT = 6
NSA_SEL_TOPN = 16
NSA_WINDOW = 512
NSA_PHI_HIDDEN = 128
NSA_SCALE = DH ** -0.5

D_FF = 4 * D_MODEL

IN_WIDTHS = (HEADS * DH, MOBA_KV_HEADS * DH, MOBA_KV_HEADS * DH, MLA_Q_LORA, MLA_KV_LORA, MLA_ROPE,
             HEADS * DH, DH, DH, DH, DH, DH, DH, 3 * HEADS)
(S_MQ, S_MK, S_MV, S_CQ, S_CKV, S_KR, S_NQ, S_KC, S_VC, S_KS, S_VS, S_KW, S_VW, S_NG) = (
    int(v) for v in np.concatenate([[0], np.cumsum(IN_WIDTHS)[:-1]]))

G_MQ = 0
G_MKV = 1024
G_CQ = 1280
G_CKV = 1536
G_KR = 1664
G_NQ = 1792
G_NSA = 2816
G_WIN = 3072
G_NG = 3200
W_ALL_COLS = 3328

VMEM_LIMIT = 56 * 1024 * 1024

Cfg = collections.namedtuple("Cfg", "n_seq t_len dec_batch dec_seq past_len page")
PROD = Cfg(BATCH, SEQ, DEC_BATCH, DEC_SEQ, PAST_LEN, PAGE_SIZE)


def _params(n_axes):
    return pltpu.CompilerParams(dimension_semantics=("arbitrary",) * n_axes, vmem_limit_bytes=VMEM_LIMIT)


def _rms(x, g):
    return x * lax.rsqrt(jnp.mean(x * x, axis=-1, keepdims=True) + NORM_EPS) * g


def _mm(a, b):
    return jnp.dot(a.astype(BF16), b.astype(BF16), preferred_element_type=F32)


def _mm_nt(a, b):
    return lax.dot_general(a.astype(BF16), b.astype(BF16), (((1,), (1,)), ((), ())),
                           preferred_element_type=F32)


def _sigmoid(x):
    return 1.0 / (1.0 + jnp.exp(-x))


def _tile_lanes(t, n):
    return t if n == 1 else jnp.concatenate([t] * n, axis=1)


def _tile_rows(t, n):
    return t if n == 1 else jnp.concatenate([t] * n, axis=0)


def _iota(shape, dim):
    return lax.broadcasted_iota(jnp.int32, shape, dim)


def _const_spec(shape):
    nd = len(shape)
    return pl.BlockSpec(shape, lambda *_: (0,) * nd)


def _in_proj_kernel(x_ref, g_ref, wall_ref, qn_ref, wuq_ref, wcat_ref, kvn_ref,
                    ca_ref, sa_ref, cb_ref, sb_ref,
                    mq_ref, mkv_ref, qcat_ref, kcat_ref, nqraw_ref, nqrot_ref, cmp_ref, slc_ref, win_ref, gate_ref,
                    *, kv_group_major):
    hb = _rms(x_ref[...], g_ref[...]).astype(BF16)

    def proj(lo, width):
        return jnp.dot(hb, wall_ref[:, lo:lo + width], preferred_element_type=F32)

    ca, sa, cb, sb = ca_ref[...], sa_ref[...], cb_ref[...], sb_ref[...]
    rows = hb.shape[0]
    lane = _iota((rows, LANES), 1)

    def rotary(z, cos, sin, width, half, lanes=None):
        first = (lane & (width - 1)) < half
        out = []
        for b in range(z.shape[1] // LANES):
            zb = z[:, b * LANES:(b + 1) * LANES]
            keep = lanes(b)
            if keep is False:
                out.append(zb)
                continue
            partner = jnp.where(first, pltpu.roll(zb, LANES - half, 1), pltpu.roll(zb, half, 1))
            rb = zb * cos + partner * sin
            out.append(rb if keep is True else jnp.where(keep, rb, zb))
        return out[0] if len(out) == 1 else jnp.concatenate(out, axis=1)

    key_half = lambda b: lane < DH
    rope16 = lambda z, lanes=lambda b: True: rotary(z, ca, sa, DH, ROT16 // 2, lanes)
    mq_ref[...] = rope16(proj(G_MQ, 1024))
    mkv_ref[...] = rope16(proj(G_MKV, 256), key_half if kv_group_major else (lambda b: b == 0))

    cqn = _rms(proj(G_CQ, MLA_Q_LORA), qn_ref[...]).astype(BF16)
    qall = jnp.dot(cqn, wuq_ref[...], preferred_element_type=F32)
    rope32 = lambda z: rotary(z, cb, sb, MLA_ROPE, MLA_ROPE // 2, lambda b: True)
    qin = jnp.concatenate([qall[:, :512], rope32(qall[:, 512:768])], axis=1).astype(BF16)
    qcat_ref[...] = jnp.dot(qin, wcat_ref[...], preferred_element_type=F32)
    kcat_ref[:, 0:LANES] = _rms(proj(G_CKV, MLA_KV_LORA), kvn_ref[...])
    kcat_ref[:, LANES:2 * LANES] = jnp.where(lane < MLA_ROPE, rope32(proj(G_KR, LANES)), 0.0)

    nq = proj(G_NQ, 1024)
    nqraw_ref[...] = nq
    nqrot_ref[...] = rope16(nq)
    cmp_ref[...] = proj(G_NSA, LANES)
    slc_ref[...] = rope16(proj(G_NSA + LANES, LANES), key_half)
    win_ref[...] = rope16(proj(G_WIN, LANES), key_half)
    gate_ref[...] = _sigmoid(proj(G_NG, LANES))


def in_proj(x, g_mix, w_all, q_norm, w_uq_all, w_cat, kv_norm, tabs, *, tm, kv_group_major):
    m = x.shape[0]
    ca, sa, cb, sb = tabs
    n_per = ca.shape[0] // tm
    row = lambda w: pl.BlockSpec((tm, w), lambda i: (i, 0))
    tab = pl.BlockSpec((tm, LANES), lambda i: (i % n_per, 0))
    widths = (1024, 256, HEADS * MLA_QW, 256, 1024, 1024, LANES, LANES, LANES, LANES)
    return pl.pallas_call(
        functools.partial(_in_proj_kernel, kv_group_major=kv_group_major),
        grid=(m // tm,),
        in_specs=[row(D_MODEL), _const_spec((1, D_MODEL)), _const_spec((D_MODEL, W_ALL_COLS)),
                  _const_spec((1, MLA_Q_LORA)), _const_spec((MLA_Q_LORA, 768)),
                  _const_spec((768, HEADS * MLA_QW)), _const_spec((1, MLA_KV_LORA)), tab, tab, tab, tab],
        out_specs=[row(w) for w in widths],
        out_shape=[jax.ShapeDtypeStruct((m, w), F32) for w in widths],
        compiler_params=_params(1),
        name="in_proj",
    )(x, g_mix, w_all, q_norm, w_uq_all, w_cat, kv_norm, ca, sa, cb, sb)


def _merge_kernel(x_ref, g_ref, wgate_ref, oa_ref, ob_ref, oc_ref, wuv_ref, woa_ref, wob_ref, woc_ref,
                  wout_ref, out_ref):
    x = x_ref[...]
    hb = _rms(x, g_ref[...]).astype(BF16)

    def gate(k):
        return _sigmoid(jnp.dot(hb, wgate_ref[:, k * D_MODEL:(k + 1) * D_MODEL], preferred_element_type=F32))

    o_b = _mm(ob_ref[...], wuv_ref[...])
    merged = gate(0) * _mm(oa_ref[...], woa_ref[...])
    merged = merged + gate(1) * _mm(o_b, wob_ref[...])
    merged = merged + gate(2) * _mm(oc_ref[...], woc_ref[...])
    out_ref[...] = x + _mm(merged, wout_ref[...])


def merge(x, g_mix, w_gate, o_a, o_b_lat, o_c, w_uv_bd, w_oa, w_ob, w_oc, w_out, *, tm):
    m = x.shape[0]
    row = lambda w: pl.BlockSpec((tm, w), lambda i: (i, 0))
    return pl.pallas_call(
        _merge_kernel,
        grid=(m // tm,),
        in_specs=[row(D_MODEL), _const_spec((1, D_MODEL)), _const_spec((D_MODEL, 3 * D_MODEL)),
                  row(1024), row(1024), row(1024), _const_spec((1024, 512)),
                  _const_spec((1024, D_MODEL)), _const_spec((512, D_MODEL)), _const_spec((1024, D_MODEL)),
                  _const_spec((D_MODEL, D_MODEL))],
        out_specs=row(D_MODEL),
        out_shape=jax.ShapeDtypeStruct((m, D_MODEL), F32),
        compiler_params=_params(1),
        name="merge",
    )(x, g_mix, w_gate, o_a, o_b_lat, o_c, w_uv_bd, w_oa, w_ob, w_oc, w_out)


def _ffn_kernel(x_ref, g_ref, w1_ref, w2_ref, gf_ref, out_ref, *, final_norm):
    x = x_ref[...]
    hb = _rms(x, g_ref[...]).astype(BF16)
    acc = x
    for c in range(D_FF // D_MODEL):
        u = jnp.maximum(jnp.dot(hb, w1_ref[:, c * D_MODEL:(c + 1) * D_MODEL], preferred_element_type=F32), 0.0)
        acc = acc + jnp.dot((u * u).astype(BF16), w2_ref[c * D_MODEL:(c + 1) * D_MODEL, :],
                            preferred_element_type=F32)
    out_ref[...] = _rms(acc, gf_ref[...]) if final_norm else acc


def ffn(x, g_ffn, w1, w2, g_final, *, tm, final_norm):
    m = x.shape[0]
    row = pl.BlockSpec((tm, D_MODEL), lambda i: (i, 0))
    return pl.pallas_call(
        functools.partial(_ffn_kernel, final_norm=final_norm),
        grid=(m // tm,),
        in_specs=[row, _const_spec((1, D_MODEL)), _const_spec((D_MODEL, D_FF)), _const_spec((D_FF, D_MODEL)),
                  _const_spec((1, D_MODEL))],
        out_specs=row,
        out_shape=jax.ShapeDtypeStruct((m, D_MODEL), F32),
        compiler_params=_params(1),
        name="ffn",
    )(x, g_ffn, w1, w2, g_final)


def _stack_heads(ref, width):
    return jnp.concatenate([ref[:, h * width:(h + 1) * width] for h in range(HEADS)], axis=0)


def _unstack_heads(out_ref, o, rows):
    for h in range(HEADS):
        out_ref[:, h * LANES:(h + 1) * LANES] = o[h * rows:(h + 1) * rows, :]


def _softmax_parts(parts):
    m = None
    for s, mask, _, _ in parts:
        mt = jnp.max(jnp.where(mask, s, NEG_INF), axis=1, keepdims=True)
        m = mt if m is None else jnp.maximum(m, mt)
    l, es = None, []
    for s, mask, _, _ in parts:
        e = jnp.where(mask, jnp.exp(s - m), 0.0)
        es.append(e)
        lt = jnp.sum(e, axis=1, keepdims=True)
        l = lt if l is None else l + lt
    inv = 1.0 / jnp.maximum(l, 1e-30)
    acc, ps = None, []
    for e, (_, _, v, v_t) in zip(es, parts):
        p = e * inv
        ps.append(p)
        o = _mm_nt(p, v) if v_t else _mm(p, v)
        acc = o if acc is None else acc + o
    return acc, ps


def _attend_once(q, k_op, v_op, vis):
    s_all = _mm_nt(q, k_op)
    ps = []
    for c in range(q.shape[0] // ROW_CHUNK):
        s = s_all[c * ROW_CHUNK:(c + 1) * ROW_CHUNK]
        m = jnp.max(jnp.where(vis, s, NEG_INF), axis=1, keepdims=True)
        e = jnp.where(vis, jnp.exp(s - m), 0.0)
        ps.append(e * (1.0 / jnp.maximum(jnp.sum(e, axis=1, keepdims=True), 1e-30)))
    return _mm(jnp.concatenate([p.astype(BF16) for p in ps], axis=0), v_op), ps


def _online_init(m_sc, l_sc, acc_sc):
    m_sc[...] = jnp.full(m_sc.shape, NEG_INF, F32)
    l_sc[...] = jnp.zeros(l_sc.shape, F32)
    acc_sc[...] = jnp.zeros(acc_sc.shape, F32)


def _flash_update(m_sc, l_sc, acc_sc, q, k_op, v_op, vis=None):
    m_all, l_all = m_sc[...], l_sc[...]
    s_all = _mm_nt(q, k_op)
    reps = k_op.shape[0] // LANES
    m_out, l_out, alphas, es = [], [], [], []
    for c in range(q.shape[0] // ROW_CHUNK):
        r = slice(c * ROW_CHUNK, (c + 1) * ROW_CHUNK)
        s = s_all[r]
        if vis is not None:
            s = jnp.where(vis(c), s, NEG_INF)
        m_old = m_all[r]
        m_new = jnp.maximum(m_old, jnp.max(s, axis=1, keepdims=True))
        e = jnp.exp2(s - _tile_lanes(m_new, reps))
        alpha = jnp.exp2(m_old - m_new)
        m_out.append(m_new)
        l_out.append(alpha * l_all[r] + jnp.sum(e, axis=1, keepdims=True))
        alphas.append(alpha)
        es.append(e.astype(BF16))
    m_sc[...] = jnp.concatenate(m_out, axis=0)
    l_sc[...] = jnp.concatenate(l_out, axis=0)
    acc_sc[...] = jnp.concatenate(alphas, axis=0) * acc_sc[...] + _mm(jnp.concatenate(es, axis=0), v_op)


def _block_bias_keys(kv, key_block):
    lane = _iota(kv.shape, 1)
    return jnp.where(lane < DH, kv, jnp.where(lane - DH == key_block, NEG_INF, 0.0)).astype(BF16)


def _online_finish(l_sc, acc_sc):
    return acc_sc[...] / jnp.maximum(l_sc[...], 1e-30)


def _rank_before(score, n_cols):
    lane = _iota(score.shape, 1)
    cnt = jnp.zeros(score.shape, F32)
    for j in range(n_cols):
        col = score[:, j:j + 1]
        ahead = (col > score) | ((col == score) & (lane > j))
        cnt = cnt + jnp.where(ahead, 1.0, 0.0)
    return cnt


def _mla_prompt_kernel(q_ref, k_ref, o_ref, m_sc, l_sc, acc_sc, *, tq, tk):
    i = pl.program_id(1)
    q = (_stack_heads(q_ref, MLA_QW) * (MLA_SCALE * LOG2E)).astype(BF16)
    _online_init(m_sc, l_sc, acc_sc)

    def tile(j, vis):
        kt = k_ref[pl.ds(pl.multiple_of(j * tk, tk), tk), :].astype(BF16)
        _flash_update(m_sc, l_sc, acc_sc, q, kt, kt[:, :MLA_KV_LORA], vis)

    diag = (i * tq) // tk
    causal = diag * tk + _iota((tq, tk), 1) <= i * tq + _iota((tq, tk), 0)
    tile(diag, lambda c: causal)
    lax.fori_loop(0, diag, lambda j, carry: tile(j, None) or carry, 0)
    _unstack_heads(o_ref, _online_finish(l_sc, acc_sc), tq)


def mla_prompt(qcat, kcat, *, tq=ROW_CHUNK, tk=512):
    n, t, _ = qcat.shape
    tk = min(tk, t)
    assert tq == ROW_CHUNK and tk % tq == 0
    rows = HEADS * tq
    return pl.pallas_call(
        functools.partial(_mla_prompt_kernel, tq=tq, tk=tk),
        grid=(n, t // tq),
        in_specs=[pl.BlockSpec((None, tq, HEADS * MLA_QW), lambda b, i: (b, i, 0)),
                  pl.BlockSpec((None, t, MLA_QW), lambda b, i: (b, 0, 0))],
        out_specs=pl.BlockSpec((None, tq, HEADS * LANES), lambda b, i: (b, i, 0)),
        out_shape=jax.ShapeDtypeStruct((n, t, HEADS * LANES), F32),
        scratch_shapes=[pltpu.VMEM((rows, LANES), F32), pltpu.VMEM((rows, LANES), F32),
                        pltpu.VMEM((rows, MLA_KV_LORA), F32)],
        compiler_params=_params(2),
        name="mla_prompt",
    )(qcat, kcat)


def _moba_prompt_kernel(q_ref, kv_ref, o_ref, m_sc, l_sc, acc_sc, *, n_blocks):
    i = pl.program_id(2)
    tq = MOBA_BLOCK
    rows = MOBA_HPG * tq
    q = jnp.concatenate([q_ref[:, h * LANES:(h + 1) * LANES] for h in range(MOBA_HPG)], axis=0)
    nb_pad = -(-n_blocks // 8) * 8
    k_mean = jnp.concatenate(
        [jnp.mean(kv_ref[j * tq:(j + 1) * tq, :], axis=0, keepdims=True) for j in range(n_blocks)]
        + ([jnp.zeros((nb_pad - n_blocks, LANES), F32)] if nb_pad > n_blocks else []), axis=0)
    blk = _iota((nb_pad, rows), 0)
    past = blk < i
    s_gate = jnp.where(past, _mm_nt(k_mean, q), NEG_INF)
    cnt = jnp.zeros((nb_pad, rows), F32)
    for j in range(n_blocks):
        other = s_gate[j:j + 1, :]
        cnt = cnt + jnp.where((other > s_gate) | ((other == s_gate) & (blk > j)), 1.0, 0.0)
    not_sel = jnp.where((blk < n_blocks) & ~(past & (cnt < MOBA_TOPK)), 1.0, 0.0)
    flags = jnp.concatenate([jnp.zeros((DH, rows), F32), not_sel, jnp.zeros((LANES - DH - nb_pad, rows), F32)],
                            axis=0).T
    qs = q * (MOBA_SCALE * LOG2E)
    qb = qs.astype(BF16)
    q_aug = (qs + flags).astype(BF16)

    _online_init(m_sc, l_sc, acc_sc)
    chunks_per_head = tq // ROW_CHUNK
    q_in = _iota((ROW_CHUNK, tq), 0)
    col = _iota((ROW_CHUNK, tq), 1)
    kv_own = kv_ref[pl.ds(pl.multiple_of(i * tq, tq), tq), :].astype(BF16)
    _flash_update(m_sc, l_sc, acc_sc, qb, kv_own, kv_own,
                  lambda c: col <= q_in + (c % chunks_per_head) * ROW_CHUNK)

    def past_blocks(first, count):
        kv_j = kv_ref[pl.ds(pl.multiple_of(first * tq, tq), count * tq), :]
        key_block = first + (_iota((count * tq, LANES), 0) >> (tq.bit_length() - 1))
        _flash_update(m_sc, l_sc, acc_sc, q_aug, _block_bias_keys(kv_j, key_block), kv_j.astype(BF16))

    lax.fori_loop(0, i // 2, lambda jj, carry: past_blocks(2 * jj, 2) or carry, 0)

    @pl.when(i % 2 == 1)
    def _():
        past_blocks(i - 1, 1)
    o = _online_finish(l_sc, acc_sc)
    for h in range(MOBA_HPG):
        o_ref[:, h * LANES:(h + 1) * LANES] = o[h * tq:(h + 1) * tq, :]


def moba_prompt(mq, mkv):
    n, t, _ = mq.shape
    n_blocks = t // MOBA_BLOCK
    rows = MOBA_HPG * MOBA_BLOCK
    gw = MOBA_HPG * LANES
    return pl.pallas_call(
        functools.partial(_moba_prompt_kernel, n_blocks=n_blocks),
        grid=(n, MOBA_KV_HEADS, n_blocks),
        in_specs=[pl.BlockSpec((None, MOBA_BLOCK, gw), lambda b, g, i: (b, i, g)),
                  pl.BlockSpec((None, t, LANES), lambda b, g, i: (b, 0, g))],
        out_specs=pl.BlockSpec((None, MOBA_BLOCK, gw), lambda b, g, i: (b, i, g)),
        out_shape=jax.ShapeDtypeStruct((n, t, HEADS * LANES), F32),
        scratch_shapes=[pltpu.VMEM((rows, LANES), F32), pltpu.VMEM((rows, LANES), F32), pltpu.VMEM((rows, LANES), F32)],
        compiler_params=_params(3),
        name="moba_prompt",
    )(mq, mkv)


def _compress(load_rows, n_chunks, pe_ref, w1_ref, w2_ref):
    lo = jnp.zeros((n_chunks, 2 * NSA_PHI_HIDDEN), F32)
    hi = jnp.zeros((n_chunks, 2 * NSA_PHI_HIDDEN), F32)
    half = NSA_CMP_STRIDE // 2
    for j in range(half):
        x = jnp.concatenate([load_rows(2 * j), load_rows(2 * j + 1)], axis=1)
        lo = lo + _mm(x + pe_ref[j:j + 1, :], w1_ref[j])
        hi = hi + _mm(x + pe_ref[half + j:half + j + 1, :], w1_ref[half + j])
    hidden = lo + pltpu.roll(hi, n_chunks - 1, 0)
    return _mm(hidden * _sigmoid(hidden), w2_ref[...])


def _nsa_compress_kernel(rows_ref, pe_ref, w1_ref, w2_ref, out_ref, *, n_chunks):
    out_ref[...] = _compress(lambda r: rows_ref[pl.ds(r, n_chunks, stride=NSA_CMP_STRIDE), :],
                             n_chunks, pe_ref, w1_ref, w2_ref)


def nsa_compress_prompt(cmp_rows, pe, w1, w2):
    n, t, _ = cmp_rows.shape
    n_chunks = t // NSA_CMP_STRIDE
    return pl.pallas_call(
        functools.partial(_nsa_compress_kernel, n_chunks=n_chunks),
        grid=(n,),
        in_specs=[pl.BlockSpec((None, t, LANES), lambda b: (b, 0, 0)), _const_spec(pe.shape),
                  _const_spec(w1.shape), _const_spec(w2.shape)],
        out_specs=pl.BlockSpec((None, n_chunks, LANES), lambda b: (b, 0, 0)),
        out_shape=jax.ShapeDtypeStruct((n, n_chunks, LANES), F32),
        compiler_params=_params(1),
        name="nsa_compress",
    )(cmp_rows, pe, w1, w2)


def _nsa_select(p_sum, q_pos, ovl_ref, rows_q, width, n_sel_blocks):
    imp = _mm(p_sum, ovl_ref[...])
    blk = _iota((rows_q, width), 1)
    cur = q_pos >> SEL_SHIFT
    forced = (blk == 0) | (blk == cur) | (blk == cur - 1)
    avail = (blk * NSA_SEL_BLOCK <= q_pos) & (blk < n_sel_blocks)
    score = jnp.where(avail, jnp.where(forced, SEL_FORCE, imp), NEG_INF)
    return avail & (_rank_before(score, n_sel_blocks) < NSA_SEL_TOPN)


def _nsa_prompt_kernel(qraw_ref, qrot_ref, gate_ref, ckcv_ref, rows_ref, win_ref, ovl_ref, o_ref,
                       m_sc, l_sc, acc_sc, *, tq, tk, n_cmp, n_sel_blocks):
    i = pl.program_id(1)
    nc_pad = ckcv_ref.shape[0]
    gate = gate_ref[...]
    q_raw = (_stack_heads(qraw_ref, LANES) * NSA_SCALE).astype(BF16)
    q_rot = _stack_heads(qrot_ref, LANES) * NSA_SCALE

    ckcv = ckcv_ref[...].astype(BF16)
    c_idx = _iota((tq, nc_pad), 1)
    vis_c = (c_idx * NSA_CMP_STRIDE + (NSA_CMP_LEN - 1) <= i * tq + _iota((tq, nc_pad), 0)) & (c_idx < n_cmp)
    o_c, p_c = _attend_once(q_raw, ckcv, ckcv, vis_c)
    p_sum = p_c[0]
    for h in range(1, HEADS):
        p_sum = p_sum + p_c[h]

    width = ovl_ref.shape[1]
    blk = _iota((tq, width), 1)
    sel = _nsa_select(p_sum, i * tq + _iota((tq, width), 0), ovl_ref, tq, width, n_sel_blocks)
    flags = pltpu.roll(jnp.where((blk < n_sel_blocks) & ~sel, 1.0, 0.0), DH, 1)
    q_aug = (q_rot * LOG2E + _tile_rows(flags, HEADS)).astype(BF16)
    _online_init(m_sc, l_sc, acc_sc)

    def tile(t, vis):
        kv = rows_ref[pl.ds(pl.multiple_of(t * tk, tk), tk), :]
        key_block = (t * tk + _iota((tk, LANES), 0)) >> SEL_SHIFT
        _flash_update(m_sc, l_sc, acc_sc, q_aug, _block_bias_keys(kv, key_block), kv.astype(BF16), vis)

    diag = (i * tq) // tk
    causal = diag * tk + _iota((tq, tk), 1) <= i * tq + _iota((tq, tk), 0)
    tile(diag, lambda c: causal)
    lax.fori_loop(0, diag, lambda t, carry: tile(t, None) or carry, 0)
    o_s = _online_finish(l_sc, acc_sc)

    slab = NSA_WINDOW + tq
    start = jnp.maximum(i * tq - NSA_WINDOW, 0)
    kv_w = win_ref[pl.ds(pl.multiple_of(start, tq), slab), :].astype(BF16)
    dist = i * tq + _iota((tq, slab), 0) - (start + _iota((tq, slab), 1))
    o_w, _ = _attend_once(q_rot.astype(BF16), kv_w, kv_w, (dist >= 0) & (dist <= NSA_WINDOW))

    for h in range(HEADS):
        r = slice(h * tq, (h + 1) * tq)
        o_ref[:, h * LANES:(h + 1) * LANES] = (gate[:, h:h + 1] * o_c[r] + gate[:, HEADS + h:HEADS + h + 1] * o_s[r]
                                               + gate[:, 2 * HEADS + h:2 * HEADS + h + 1] * o_w[r])


def _overlap_matrix(nc_pad, n_cmp, width):
    start = np.arange(nc_pad)[:, None] * NSA_CMP_STRIDE
    j0 = np.arange(width)[None, :] * NSA_SEL_BLOCK
    ovl = (start < j0 + NSA_SEL_BLOCK) & (start + NSA_CMP_LEN > j0) & (np.arange(nc_pad)[:, None] < n_cmp)
    return jnp.asarray(ovl, BF16)


def nsa_prompt(nq_raw, nq_rot, gate, ckcv, slc_rows, win_rows, *, tq=ROW_CHUNK, tk=512):
    n, t, _ = nq_raw.shape
    tk = min(tk, t)
    nc_pad = t // NSA_CMP_STRIDE
    n_cmp = (t - NSA_CMP_LEN) // NSA_CMP_STRIDE + 1
    n_sel_blocks = t // NSA_SEL_BLOCK
    assert tq == ROW_CHUNK and tk % tq == 0 and n_sel_blocks <= DH
    assert t >= NSA_WINDOW + tq and NSA_WINDOW % tq == 0
    width = LANES
    ovl = _overlap_matrix(nc_pad, n_cmp, width)
    rows = HEADS * tq
    qspec = pl.BlockSpec((None, tq, HEADS * LANES), lambda b, i: (b, i, 0))
    return pl.pallas_call(
        functools.partial(_nsa_prompt_kernel, tq=tq, tk=tk, n_cmp=n_cmp, n_sel_blocks=n_sel_blocks),
        grid=(n, t // tq),
        in_specs=[qspec, qspec, pl.BlockSpec((None, tq, LANES), lambda b, i: (b, i, 0)),
                  pl.BlockSpec((None, nc_pad, LANES), lambda b, i: (b, 0, 0)),
                  pl.BlockSpec((None, t, LANES), lambda b, i: (b, 0, 0)),
                  pl.BlockSpec((None, t, LANES), lambda b, i: (b, 0, 0)),
                  _const_spec(ovl.shape)],
        out_specs=qspec,
        out_shape=jax.ShapeDtypeStruct((n, t, HEADS * LANES), F32),
        scratch_shapes=[pltpu.VMEM((rows, LANES), F32), pltpu.VMEM((rows, LANES), F32), pltpu.VMEM((rows, LANES), F32)],
        compiler_params=_params(2),
        name="nsa_prompt",
    )(nq_raw, nq_rot, gate, ckcv, slc_rows, win_rows, ovl)


def _pad_rows(x, n):
    return jnp.concatenate([x, jnp.zeros((n - x.shape[0], x.shape[1]), x.dtype)], axis=0)


def _all_pages(pages, lo, hi):
    return jnp.concatenate([pg[lo:hi, :] for pg in pages], axis=1).astype(BF16)


def _block_bias_matrix(width, n_blocks, block, n_tokens):
    blk = np.arange(width)[:, None]
    tok = np.arange(n_tokens)[None, :]
    return jnp.asarray(np.where((tok // block == blk) & (blk < n_blocks), NEG_INF, 0.0), BF16)


def _paged_attend(s_past, v_t, s_new, vis_new, v_new, bias=None):
    ps, ps_new = [], []
    for c in range(s_past.shape[0] // 8):
        r = slice(8 * c, 8 * c + 8)
        sp, vn = s_past[r] if bias is None else s_past[r] + bias, vis_new[r]
        sn = jnp.where(vn, s_new[r], NEG_INF)
        m = jnp.maximum(jnp.max(sp, axis=1, keepdims=True), jnp.max(sn, axis=1, keepdims=True))
        e, en = jnp.exp(sp - m), jnp.where(vn, jnp.exp(sn - m), 0.0)
        inv = 1.0 / jnp.maximum(jnp.sum(e, axis=1, keepdims=True) + jnp.sum(en, axis=1, keepdims=True), 1e-30)
        ps.append((e * inv).astype(BF16))
        ps_new.append((en * inv).astype(BF16))
    return _mm_nt(jnp.concatenate(ps, axis=0), v_t) + _mm(jnp.concatenate(ps_new, axis=0), v_new)


def _new_token_vis(rows, s_len):
    return _iota((rows, LANES), 1) <= (_iota((rows, LANES), 0) & (s_len - 1))


SEQ_PER_STEP = 1
NSA_SEQ_PER_STEP = 2


def _page_specs(sps, n_pages, rows, layer):
    return [pl.BlockSpec((None, None, rows, LANES),
                         functools.partial(lambda i, pt, u, p: (layer, pt[sps * i + u, p], 0, 0), u=u, p=p))
            for u in range(sps) for p in range(n_pages)]


def _seq_spec(sps, s_len, width):
    return pl.BlockSpec((sps, s_len, width), lambda i, pt: (i, 0, 0))


def _per_sequence(body, sps, n_seq_in, n_const, n_pages):
    def kernel(pt_ref, *refs):
        seq_in, refs = refs[:n_seq_in], refs[n_seq_in:]
        consts, refs = refs[:n_const], refs[n_const:]
        pages, refs = refs[:sps * n_pages], refs[sps * n_pages:]
        for u in range(sps):
            body(pt_ref, *[r.at[u] for r in seq_in], *consts, *pages[u * n_pages:(u + 1) * n_pages],
                 *[r.at[u] for r in refs])
    return kernel


def _moba_sample_kernel(pt_ref, q_ref, kvn_ref, bias_ref, *rest, n_pages, s_len, page):
    pages, (o_ref,) = rest[:n_pages], rest[n_pages:]
    rows = HEADS * s_len
    qb = (_stack_heads(q_ref, LANES) * MOBA_SCALE).astype(BF16)
    n_blocks = n_pages * page // MOBA_BLOCK
    s_raw = _mm(qb, _all_pages(pages, 0, LANES))
    lane = _iota((rows, LANES), 1)
    s_gate = jnp.full((rows, LANES), NEG_INF, F32)
    for blk in range(n_blocks):
        tot = jnp.sum(s_raw[:, blk * MOBA_BLOCK:(blk + 1) * MOBA_BLOCK], axis=1, keepdims=True)
        s_gate = jnp.where(lane == blk, tot * (1.0 / MOBA_BLOCK), s_gate)
    not_sel = jnp.where((lane < n_blocks) & ~(_rank_before(s_gate, n_blocks) < MOBA_TOPK), 1.0, 0.0)

    kv_new = _pad_rows(kvn_ref[...], LANES).astype(BF16)
    o = _paged_attend(s_raw + _mm(not_sel, bias_ref[...]), _all_pages(pages, LANES, 2 * LANES),
                      _mm_nt(qb, kv_new[:, 0:LANES]), _new_token_vis(rows, s_len), kv_new[:, LANES:2 * LANES])
    _unstack_heads(o_ref, o, s_len)


def moba_sample(mq, mkv_new, cache_t, page_table, *, layer):
    b, s_len, _ = mq.shape
    n_pages = page_table.shape[1]
    page = cache_t.shape[3]
    assert (n_pages * page) % MOBA_BLOCK == 0 and MOBA_BLOCK % page == 0 and s_len <= LANES
    n_blocks = n_pages * page // MOBA_BLOCK
    assert MOBA_TOPK <= n_blocks <= LANES
    bias = _block_bias_matrix(LANES, n_blocks, MOBA_BLOCK, n_pages * page)
    sps = SEQ_PER_STEP
    assert b % sps == 0
    body = functools.partial(_moba_sample_kernel, n_pages=n_pages, s_len=s_len, page=page)
    return pl.pallas_call(
        _per_sequence(body, sps, 2, 1, n_pages),
        grid_spec=pltpu.PrefetchScalarGridSpec(
            num_scalar_prefetch=1, grid=(b // sps,),
            in_specs=[_seq_spec(sps, s_len, HEADS * LANES), _seq_spec(sps, s_len, 256),
                      pl.BlockSpec(bias.shape, lambda i, pt: (0, 0))] + _page_specs(sps, n_pages, 256, layer),
            out_specs=_seq_spec(sps, s_len, HEADS * LANES)),
        out_shape=jax.ShapeDtypeStruct((b, s_len, HEADS * LANES), F32),
        compiler_params=_params(1),
        name="moba_sample",
    )(page_table, mq, mkv_new, bias, *([cache_t] * (sps * n_pages)))


def _mla_sample_kernel(pt_ref, q_ref, kn_ref, *rest, n_pages, s_len, page):
    pages, (o_ref,) = rest[:n_pages], rest[n_pages:]
    rows = HEADS * s_len
    qb = (_stack_heads(q_ref, MLA_QW) * MLA_SCALE).astype(BF16)
    k_new = _pad_rows(kn_ref[...], LANES).astype(BF16)
    rows_t = _all_pages(pages, 0, MLA_ROW)
    o = _paged_attend(_mm(qb[:, 0:MLA_ROW], rows_t), rows_t[0:MLA_KV_LORA, :], _mm_nt(qb, k_new),
                      _new_token_vis(rows, s_len), k_new[:, 0:MLA_KV_LORA])
    _unstack_heads(o_ref, o, s_len)


def mla_sample(qcat, kcat_new, cache_t, page_table, *, layer):
    b, s_len, _ = qcat.shape
    n_pages = page_table.shape[1]
    page = cache_t.shape[3]
    sps = SEQ_PER_STEP
    assert page == LANES and s_len <= LANES and b % sps == 0
    body = functools.partial(_mla_sample_kernel, n_pages=n_pages, s_len=s_len, page=page)
    return pl.pallas_call(
        _per_sequence(body, sps, 2, 0, n_pages),
        grid_spec=pltpu.PrefetchScalarGridSpec(
            num_scalar_prefetch=1, grid=(b // sps,),
            in_specs=[_seq_spec(sps, s_len, HEADS * MLA_QW), _seq_spec(sps, s_len, MLA_QW)]
            + _page_specs(sps, n_pages, MLA_ROW, layer),
            out_specs=_seq_spec(sps, s_len, HEADS * LANES)),
        out_shape=jax.ShapeDtypeStruct((b, s_len, HEADS * LANES), F32),
        compiler_params=_params(1),
        name="mla_sample",
    )(page_table, qcat, kcat_new, *([cache_t] * (sps * n_pages)))


def _nsa_sample_kernel(pt_ref, qraw_ref, qrot_ref, gate_ref, slcn_ref, winn_ref, win_ref, pe_ref, w1_ref, w2_ref,
                       ovl_ref, *rest, n_pages, s_len, page, past_len):
    pages, (o_ref, xs_sc) = rest[:n_pages], rest[n_pages:]
    rows = HEADS * s_len
    q_raw = (_stack_heads(qraw_ref, LANES) * NSA_SCALE).astype(BF16)
    q_rot = (_stack_heads(qrot_ref, LANES) * NSA_SCALE).astype(BF16)

    for p in range(n_pages):
        xs_sc[p * page:(p + 1) * page, :] = pages[p][0:LANES, :].T
    n_chunks = past_len // NSA_CMP_STRIDE
    n_cmp = n_chunks - 1
    ckcv = _compress(lambda r: xs_sc[pl.ds(r, n_chunks, stride=NSA_CMP_STRIDE), :],
                     n_chunks, pe_ref, w1_ref, w2_ref).astype(BF16)
    q_pos_c = past_len + (_iota((rows, n_chunks), 0) & (s_len - 1))
    c_idx = _iota((rows, n_chunks), 1)
    vis_c = (c_idx * NSA_CMP_STRIDE + (NSA_CMP_LEN - 1) <= q_pos_c) & (c_idx < n_cmp)
    o_c, (p_c,) = _softmax_parts([(_mm_nt(q_raw, ckcv), vis_c, ckcv, False)])

    width = ovl_ref.shape[1]
    n_sel_blocks = -(-(past_len + s_len) // NSA_SEL_BLOCK)
    q_pos1 = past_len + _iota((s_len, width), 0)
    p_sum = p_c[0:s_len, :]
    for h in range(1, HEADS):
        p_sum = p_sum + p_c[h * s_len:(h + 1) * s_len, :]
    sel = _nsa_select(p_sum, q_pos1, ovl_ref, s_len, width, n_sel_blocks)
    hide = jnp.where(sel, 0.0, NEG_INF)
    bpp = page // NSA_SEL_BLOCK
    blk_in_page = _iota((s_len, page), 1) >> SEL_SHIFT
    pieces = []
    for p in range(n_pages):
        piece = jnp.broadcast_to(hide[:, p * bpp:p * bpp + 1], (s_len, page))
        for u in range(1, bpp):
            piece = jnp.where(blk_in_page == u, jnp.broadcast_to(hide[:, p * bpp + u:p * bpp + u + 1], (s_len, page)),
                              piece)
        pieces.append(piece)
    slc_t = _all_pages(pages, LANES, 2 * LANES)
    new_blk = past_len // NSA_SEL_BLOCK
    sel_new = jnp.where(sel[:, new_blk:new_blk + 1], 1.0, 0.0)
    vis_new = _new_token_vis(rows, s_len) & (_tile_rows(jnp.broadcast_to(sel_new, (s_len, LANES)), HEADS) > 0.5)
    slc_new = _pad_rows(slcn_ref[...], LANES).astype(BF16)
    o_s = _paged_attend(_mm(q_rot, slc_t), slc_t, _mm_nt(q_rot, slc_new), vis_new, slc_new,
                        bias=jnp.concatenate(pieces, axis=1))

    wb = win_ref.shape[1]
    q_pos_w = past_len + (_iota((rows, wb), 0) & (s_len - 1))
    w_pos = past_len - wb + _iota((rows, wb), 1)
    dist = q_pos_w - w_pos
    vis_buf = (dist >= 0) & (dist <= NSA_WINDOW) & (w_pos >= 0)
    win_t = win_ref[...].astype(BF16)
    win_new = _pad_rows(winn_ref[...], LANES).astype(BF16)
    o_w, _ = _softmax_parts([(_mm(q_rot, win_t), vis_buf, win_t, True),
                             (_mm_nt(q_rot, win_new), _new_token_vis(rows, s_len), win_new, False)])

    gate = gate_ref[...]
    for h in range(HEADS):
        sl = slice(h * s_len, (h + 1) * s_len)
        o_ref[:, h * LANES:(h + 1) * LANES] = (gate[:, h:h + 1] * o_c[sl]
                                               + gate[:, HEADS + h:HEADS + h + 1] * o_s[sl]
                                               + gate[:, 2 * HEADS + h:2 * HEADS + h + 1] * o_w[sl])


def nsa_sample(nq_raw, nq_rot, gate, slc_new, win_new, cache_t, win_t, page_table, pe, w1, w2, *, layer):
    b, s_len, _ = nq_raw.shape
    n_pages = page_table.shape[1]
    page = cache_t.shape[3]
    past_len = n_pages * page
    wb = win_t.shape[3]
    assert page == LANES and page % NSA_SEL_BLOCK == 0 and past_len % NSA_CMP_STRIDE == 0
    assert s_len < NSA_CMP_STRIDE and s_len <= NSA_SEL_BLOCK and wb <= past_len and NSA_WINDOW <= wb + s_len
    n_chunks = past_len // NSA_CMP_STRIDE
    n_sel_blocks = -(-(past_len + s_len) // NSA_SEL_BLOCK)
    width = -(-n_sel_blocks // LANES) * LANES
    sps = NSA_SEQ_PER_STEP if b % NSA_SEQ_PER_STEP == 0 else 1
    assert s_len == 8
    ovl = _overlap_matrix(n_chunks, n_chunks - 1, width)
    const = lambda a: pl.BlockSpec(a.shape, lambda i, pt: (0,) * a.ndim)
    body = functools.partial(_nsa_sample_kernel, n_pages=n_pages, s_len=s_len, page=page, past_len=past_len)
    return pl.pallas_call(
        _per_sequence(body, sps, 6, 4, n_pages),
        grid_spec=pltpu.PrefetchScalarGridSpec(
            num_scalar_prefetch=1, grid=(b // sps,),
            in_specs=[_seq_spec(sps, s_len, HEADS * LANES), _seq_spec(sps, s_len, HEADS * LANES),
                      _seq_spec(sps, s_len, LANES), _seq_spec(sps, s_len, LANES), _seq_spec(sps, s_len, LANES),
                      pl.BlockSpec((None, sps, LANES, wb), lambda i, pt: (layer, i, 0, 0)),
                      const(pe), const(w1), const(w2), const(ovl)] + _page_specs(sps, n_pages, 256, layer),
            out_specs=_seq_spec(sps, s_len, HEADS * LANES),
            scratch_shapes=[pltpu.VMEM((sps, past_len, LANES), F32)]),
        out_shape=jax.ShapeDtypeStruct((b, s_len, HEADS * LANES), F32),
        compiler_params=_params(1),
        name="nsa_sample",
    )(page_table, nq_raw, nq_rot, gate, slc_new, win_new, win_t, pe, w1, w2, ovl, *([cache_t] * (sps * n_pages)))


def _w_all_columns(q_lane_offset, kv_group_major):
    src = np.full(W_ALL_COLS, -1, np.int64)
    for h in range(HEADS):
        src[G_MQ + h * LANES + q_lane_offset(h) + np.arange(DH)] = S_MQ + h * DH + np.arange(DH)
        src[G_NQ + h * LANES + np.arange(DH)] = S_NQ + h * DH + np.arange(DH)
    for g in range(MOBA_KV_HEADS):
        k_col = g * LANES if kv_group_major else g * DH
        v_col = g * LANES + DH if kv_group_major else LANES + g * DH
        src[G_MKV + k_col + np.arange(DH)] = S_MK + g * DH + np.arange(DH)
        src[G_MKV + v_col + np.arange(DH)] = S_MV + g * DH + np.arange(DH)
    src[G_CQ:G_CQ + MLA_Q_LORA] = S_CQ + np.arange(MLA_Q_LORA)
    src[G_CKV:G_CKV + MLA_KV_LORA] = S_CKV + np.arange(MLA_KV_LORA)
    src[G_KR:G_KR + MLA_ROPE] = S_KR + np.arange(MLA_ROPE)
    src[G_NSA:G_NSA + 4 * DH] = S_KC + np.arange(4 * DH)
    src[G_WIN:G_WIN + 2 * DH] = S_KW + np.arange(2 * DH)
    src[G_NG:G_NG + 3 * HEADS] = S_NG + np.arange(3 * HEADS)
    return src


def _take_static(w, src, axis):
    axis = axis % w.ndim
    pieces, i, n = [], 0, len(src)
    while i < n:
        j = i + 1
        if src[i] < 0:
            while j < n and src[j] < 0:
                j += 1
            pieces.append(jnp.zeros(w.shape[:axis] + (j - i,) + w.shape[axis + 1:], w.dtype))
        else:
            while j < n and src[j] == src[j - 1] + 1:
                j += 1
            pieces.append(lax.slice_in_dim(w, int(src[i]), int(src[i]) + (j - i), axis=axis))
        i = j
    return jnp.concatenate(pieces, axis=axis)


def _take_cols(w, src):
    return _take_static(w, src, -1)


def _pad_head_rows(w, lane_offset):
    src = np.full(HEADS * LANES, -1, np.int64)
    for h in range(HEADS):
        src[h * LANES + lane_offset(h) + np.arange(DH)] = h * DH + np.arange(DH)
    return _take_static(w, src, 1)


def _block_diag_heads(w):
    l, h, a, b = w.shape
    eye = jnp.eye(h, dtype=w.dtype)
    return (w[:, :, :, None, :] * eye[None, :, None, :, None]).reshape(l, h * a, h * b)


def _mla_query_weights(w_uq, w_uk):
    src = np.zeros(768, np.int64)
    per = MLA_NOPE + MLA_ROPE
    for h in range(HEADS):
        src[h * MLA_NOPE + np.arange(MLA_NOPE)] = h * per + np.arange(MLA_NOPE)
        src[512 + h * MLA_ROPE + np.arange(MLA_ROPE)] = h * per + MLA_NOPE + np.arange(MLA_ROPE)
    w_uq_all = _take_cols(w_uq, src)
    depth = w_uk.shape[0]
    wk = w_uk.reshape(depth, MLA_KV_LORA, HEADS, MLA_NOPE).transpose(0, 2, 3, 1)
    wk = jnp.pad(wk, ((0, 0), (0, 0), (0, 0), (0, MLA_QW - MLA_KV_LORA)))
    top = _block_diag_heads(wk)
    bottom = np.zeros((HEADS * MLA_ROPE, HEADS * MLA_QW), np.float32)
    for h in range(HEADS):
        for r in range(MLA_ROPE):
            bottom[h * MLA_ROPE + r, h * MLA_QW + MLA_KV_LORA + r] = 1.0
    w_cat = jnp.concatenate([top, jnp.broadcast_to(jnp.asarray(bottom), (depth,) + bottom.shape)], axis=1)
    return w_uq_all.astype(BF16), w_cat.astype(BF16)


def _compress_weights(pe_k, w1_k, w2_k, pe_v, w1_v, w2_v):
    depth = pe_k.shape[0]
    pe = jnp.concatenate([pe_k, pe_v], axis=-1)
    w1 = jnp.stack([w1_k.reshape(depth, NSA_CMP_LEN, DH, NSA_PHI_HIDDEN),
                    w1_v.reshape(depth, NSA_CMP_LEN, DH, NSA_PHI_HIDDEN)], axis=2)
    w1 = _block_diag_heads(w1.reshape(depth * NSA_CMP_LEN, 2, DH, NSA_PHI_HIDDEN))
    w1 = w1.reshape(depth, NSA_CMP_LEN // 2, 4 * DH, 2 * NSA_PHI_HIDDEN)
    pe = pe.reshape(depth, NSA_CMP_LEN // 2, 4 * DH)
    w2 = _block_diag_heads(jnp.stack([w2_k, w2_v], axis=1))
    return pe, w1.astype(BF16), w2.astype(BF16)


def _rope_tables(pos):
    posf = pos.astype(F32)[:, None]
    lane = np.arange(LANES)

    def tables(width, rot):
        d = lane % width
        half = rot // 2
        inv_freq = ROPE_THETA ** (-2.0 * jnp.arange(half, dtype=F32) / rot)
        ang = posf * inv_freq[jnp.asarray(d % half)]
        cos, sin = jnp.cos(ang), jnp.sin(ang)
        c = jnp.where(jnp.asarray(d < rot), cos, 1.0)
        s = jnp.where(jnp.asarray(d < half), -sin, jnp.where(jnp.asarray(d < rot), sin, 0.0))
        return c, s

    ca, sa = tables(DH, ROT16)
    cb, sb = tables(MLA_ROPE, MLA_ROPE)
    return ca, sa, cb, sb


def _forward(cfg, x_prompt, x_sample, cache_moba, cache_mla, cache_nsa, state_nsa_win, page_table,
             norm_mix, w_in, mla_q_norm, mla_w_uq, mla_kv_norm, mla_w_uk, mla_w_uv,
             nsa_pe_k, nsa_w1_k, nsa_w2_k, nsa_pe_v, nsa_w1_v, nsa_w2_v,
             w_oa, w_ob, w_oc, w_gate, w_out, norm_ffn, w_ff1, w_ff2, norm_final):
    n, t, b, s_len = cfg.n_seq, cfg.t_len, cfg.dec_batch, cfg.dec_seq
    depth = w_in.shape[0]
    mp, ms = n * t, b * s_len
    tm_p, tm_s = 256, min(256, ms)
    bf = lambda w: w.astype(BF16)
    vec = lambda g: g.reshape(depth, 1, -1)

    value_half = lambda h: DH
    group_lane = lambda h: DH * (h // MOBA_HPG)
    w_all_p = bf(_take_cols(w_in, _w_all_columns(lambda h: 0, True)))
    w_all_s = bf(_take_cols(w_in, _w_all_columns(group_lane, False)))
    w_uq_all, w_cat = _mla_query_weights(mla_w_uq, mla_w_uk)
    w_uv_bd = bf(_block_diag_heads(mla_w_uv.reshape(depth, MLA_KV_LORA, HEADS, DH).transpose(0, 2, 1, 3)))
    w_oa_p, w_oa_s = bf(_pad_head_rows(w_oa, value_half)), bf(_pad_head_rows(w_oa, group_lane))
    w_oc_pad = bf(_pad_head_rows(w_oc, value_half))
    pe, w1c, w2c = _compress_weights(nsa_pe_k, nsa_w1_k, nsa_w2_k, nsa_pe_v, nsa_w1_v, nsa_w2_v)
    w_ob_b, w_gate_b, w_out_b, w_ff1_b, w_ff2_b = bf(w_ob), bf(w_gate), bf(w_out), bf(w_ff1), bf(w_ff2)
    g_mix, g_ffn, g_q, g_kv = vec(norm_mix), vec(norm_ffn), vec(mla_q_norm), vec(mla_kv_norm)
    g_final = norm_final.reshape(1, -1)

    tabs_p = _rope_tables(jnp.arange(t))
    tabs_s = _rope_tables(cfg.past_len + (jnp.arange(tm_s) % s_len))

    pool = cache_moba.shape[1]
    cm_t = cache_moba.transpose(0, 1, 3, 4, 5, 2).reshape(depth, pool, 4 * DH, cfg.page)
    cn_t = cache_nsa.transpose(0, 1, 3, 4, 2).reshape(depth, pool, 4 * DH, cfg.page)
    cl_t = cache_mla.transpose(0, 1, 3, 2)
    win_t = state_nsa_win.transpose(0, 1, 3, 4, 2).reshape(depth, b, 2 * DH, -1)

    xp = x_prompt.reshape(mp, D_MODEL)
    xs = x_sample.reshape(ms, D_MODEL)
    leaves = [[] for _ in range(8)]
    for l in range(depth):
        last = l == depth - 1
        mq, mkv, qcat, kcat, nq_raw, nq_rot, cmp_r, slc_r, win_r, gate = in_proj(
            xp, g_mix[l], w_all_p[l], g_q[l], w_uq_all[l], w_cat[l], g_kv[l], tabs_p, tm=tm_p, kv_group_major=True)
        r3 = lambda a: a.reshape(n, t, -1)
        o_a = moba_prompt(r3(mq), r3(mkv))
        o_b = mla_prompt(r3(qcat), r3(kcat))
        ckcv = nsa_compress_prompt(r3(cmp_r), pe[l], w1c[l], w2c[l])
        o_c = nsa_prompt(r3(nq_raw), r3(nq_rot), r3(gate), ckcv, r3(slc_r), r3(win_r))
        x1 = merge(xp, g_mix[l], w_gate_b[l], o_a.reshape(mp, -1), o_b.reshape(mp, -1), o_c.reshape(mp, -1),
                   w_uv_bd[l], w_oa_p[l], w_ob_b[l], w_oc_pad[l], w_out_b[l], tm=256)
        xp = ffn(x1, g_ffn[l], w_ff1_b[l], w_ff2_b[l], g_final, tm=256, final_norm=last)
        leaves[0].append(mkv.reshape(n, t, MOBA_KV_HEADS, 2, DH).transpose(0, 1, 3, 2, 4))
        leaves[2].append(kcat[:, :MLA_ROW].reshape(n, t, MLA_ROW))
        leaves[4].append(jnp.concatenate([cmp_r, slc_r], axis=1).reshape(n, t, 4, DH))
        leaves[6].append(win_r.reshape(n, t, 2, DH)[:, t - min(NSA_WINDOW, t):])
        mq, mkv, qcat, kcat, nq_raw, nq_rot, cmp_r, slc_r, win_r, gate = in_proj(
            xs, g_mix[l], w_all_s[l], g_q[l], w_uq_all[l], w_cat[l], g_kv[l], tabs_s, tm=tm_s, kv_group_major=False)
        r3 = lambda a: a.reshape(b, s_len, -1)
        o_a = moba_sample(r3(mq), r3(mkv), cm_t, page_table, layer=l)
        o_b = mla_sample(r3(qcat), r3(kcat), cl_t, page_table, layer=l)
        o_c = nsa_sample(r3(nq_raw), r3(nq_rot), r3(gate), r3(slc_r), r3(win_r), cn_t, win_t, page_table,
                         pe[l], w1c[l], w2c[l], layer=l)
        x1 = merge(xs, g_mix[l], w_gate_b[l], o_a.reshape(ms, -1), o_b.reshape(ms, -1), o_c.reshape(ms, -1),
                   w_uv_bd[l], w_oa_s[l], w_ob_b[l], w_oc_pad[l], w_out_b[l], tm=min(256, ms))
        xs = ffn(x1, g_ffn[l], w_ff1_b[l], w_ff2_b[l], g_final, tm=min(256, ms), final_norm=last)
        leaves[1].append(mkv.reshape(b, s_len, 2, MOBA_KV_HEADS, DH))
        leaves[3].append(kcat[:, :MLA_ROW].reshape(b, s_len, MLA_ROW))
        leaves[5].append(jnp.concatenate([cmp_r, slc_r], axis=1).reshape(b, s_len, 4, DH))
        win_full = jnp.concatenate([state_nsa_win[l], win_r.reshape(b, s_len, 2, DH)], axis=1)
        leaves[7].append(win_full[:, win_full.shape[1] - min(NSA_WINDOW, win_full.shape[1]):])
    return (xp.reshape(n, t, D_MODEL), xs.reshape(b, s_len, D_MODEL)) + tuple(jnp.stack(v) for v in leaves)


def kernel(x_prompt, x_sample, cache_moba, cache_mla, cache_nsa, state_nsa_win, page_table, norm_mix, w_in,
           mla_q_norm, mla_w_uq, mla_kv_norm, mla_w_uk, mla_w_uv, nsa_pe_k, nsa_w1_k, nsa_w2_k, nsa_pe_v,
           nsa_w1_v, nsa_w2_v, w_oa, w_ob, w_oc, w_gate, w_out, norm_ffn, w_ff1, w_ff2, norm_final):
    return _forward(PROD, x_prompt, x_sample, cache_moba, cache_mla, cache_nsa, state_nsa_win, page_table,
                    norm_mix, w_in, mla_q_norm, mla_w_uq, mla_kv_norm, mla_w_uk, mla_w_uv,
                    nsa_pe_k, nsa_w1_k, nsa_w2_k, nsa_pe_v, nsa_w1_v, nsa_w2_v,
                    w_oa, w_ob, w_oc, w_gate, w_out, norm_ffn, w_ff1, w_ff2, norm_final)
```
